```python
import math
import jax
import jax.numpy as jnp
from jax import lax
import numpy as np

D_MODEL = 1024
BATCH = 2
SEQ = 16384
DEPTH = 4

HEAD_DIM = 64
SB_HEADS = 4
GLA_HEADS = 4
GLA_KEY_DIM = 32
GLA_VAL_DIM = 64
GLA_LOW_RANK = 16
GLA_GATE_NORM = 16.0
GLA_CHUNK = 16
DN_HEADS = 4
DN_CONV = 4
DN_CHUNK = 64
DIL_HEADS = 4
DIL_PATTERNS = ((128, 1), (512, 4), (2048, 16))
DIL_BLOCK = 128
SB_BLOCK = 128
N_BRANCH = 4
BRANCH_WIDTH = 4 * HEAD_DIM
ROPE_THETA = 500000.0
ROT_DIM = HEAD_DIM // 4
D_FF = 4 * D_MODEL
EPS = 1e-6

IN_SIZES = (
    SB_HEADS * HEAD_DIM, SB_HEADS * HEAD_DIM, SB_HEADS * HEAD_DIM,
    GLA_HEADS * GLA_KEY_DIM, GLA_HEADS * GLA_KEY_DIM, GLA_HEADS * GLA_VAL_DIM, GLA_LOW_RANK, GLA_HEADS * GLA_VAL_DIM,
    3 * DN_HEADS * HEAD_DIM, DN_HEADS, DN_HEADS, DN_HEADS * HEAD_DIM,
    DIL_HEADS * HEAD_DIM, DIL_HEADS * HEAD_DIM, DIL_HEADS * HEAD_DIM,
    N_BRANCH * D_MODEL,
)
N_IN = sum(IN_SIZES)
IN_OFFSETS = tuple(int(o) for o in np.cumsum(IN_SIZES)[:-1])

kernel_name = 'hybrid_sb_gla_gdn_dilated_trunk'


def rms_norm(x, g):
    xf = x.astype(jnp.float32)
    y = xf * lax.rsqrt(jnp.mean(xf * xf, axis=-1, keepdims=True) + EPS)
    return (y * g.astype(jnp.float32)).astype(x.dtype)


def l2norm(x):
    xf = x.astype(jnp.float32)
    return xf * lax.rsqrt(jnp.sum(xf * xf, axis=-1, keepdims=True) + EPS)


def rope_partial(x, pos):
    half = ROT_DIM // 2
    inv_freq = ROPE_THETA ** (-jnp.arange(half, dtype=jnp.float32) / half)
    ang = pos.astype(jnp.float32)[:, None] * inv_freq[None, :]
    cos = jnp.cos(ang)[None, :, None, :]
    sin = jnp.sin(ang)[None, :, None, :]
    x1 = x[..., :half].astype(jnp.float32)
    x2 = x[..., half:ROT_DIM].astype(jnp.float32)
    rot = jnp.concatenate([x1 * cos - x2 * sin, x2 * cos + x1 * sin], axis=-1).astype(x.dtype)
    return jnp.concatenate([rot, x[..., ROT_DIM:]], axis=-1)


def _chunk(t, c):
    B, S = t.shape[:2]
    t = t.reshape(B, S // c, c, *t.shape[2:]).astype(jnp.float32)
    return jnp.moveaxis(t, [1, 3], [0, 2])


def _unchunk(t):
    nc, B, H, c, d = t.shape
    return jnp.moveaxis(t, [0, 2], [1, 3]).reshape(B, nc * c, H, d)


def causal_conv(x, w):
    K = w.shape[-1]
    rhs = w.T[:, None, :].astype(x.dtype)
    return lax.conv_general_dilated(x, rhs, window_strides=(1,), padding=[(K - 1, 0)],
                                    dimension_numbers=('NWC', 'WIO', 'NWC'), feature_group_count=x.shape[-1])


def stick_breaking_attention(q, k, v):
    B, S, H, dh = q.shape
    nb = S // SB_BLOCK
    scale = dh ** -0.5

    def blocks(t):
        return t.reshape(B, nb, SB_BLOCK, H, dh).transpose(1, 0, 3, 2, 4)

    qb, kb, vb = blocks(q), blocks(k), blocks(v)
    pq_np, pk_np = np.tril_indices(nb)
    pq = jnp.asarray(pq_np, dtype=jnp.int32)
    pk = jnp.asarray(pk_np, dtype=jnp.int32)
    r_idx = jnp.arange(SB_BLOCK, dtype=jnp.int32)
    later = (r_idx[:, None] > r_idx[None, :]).astype(jnp.float32)

    def pair(args):
        i, j = args
        z = jnp.einsum('bhqc,bhkc->bhqk', qb[i], kb[j]).astype(jnp.float32) * scale
        causal = (j * SB_BLOCK + r_idx)[None, :] < (i * SB_BLOCK + r_idx)[:, None]
        z = jnp.where(causal, z, -jnp.inf)
        sp = jax.nn.softplus(z)
        between = -jnp.einsum('bhqk,kc->bhqc', sp, later)
        w = jnp.exp(z - sp + between)
        o = jnp.einsum('bhqk,bhkc->bhqc', w, vb[j].astype(jnp.float32))
        return o, -jnp.sum(sp, axis=-1)

    o_all, r_all = lax.map(pair, (pq, pk))
    R = jnp.zeros((nb, nb, B, H, SB_BLOCK), jnp.float32).at[pq, pk].set(r_all)
    later_blocks = lax.cumsum(R, axis=1, reverse=True) - R
    factor = jnp.exp(later_blocks[pq, pk])
    out = jax.ops.segment_sum(factor[..., None] * o_all, pq, num_segments=nb)
    return out.transpose(1, 0, 3, 2, 4).reshape(B, S, H, dh).astype(v.dtype)


def gla_chunked(q, k, v, log_a):
    B, S, H, dk = q.shape
    dv = v.shape[-1]
    c = GLA_CHUNK
    qc, kc, vc = _chunk(q, c), _chunk(k, c), _chunk(v, c)
    b = jnp.cumsum(_chunk(log_a, c), axis=-2)
    causal = jnp.tril(jnp.ones((c, c), dtype=bool))
    decay = jnp.exp(jnp.where(causal[:, :, None], b[..., :, None, :] - b[..., None, :, :], -jnp.inf))
    scores = jnp.einsum('nbhic,nbhjc,nbhijc->nbhij', qc, kc, decay)
    o_intra = jnp.einsum('nbhij,nbhjd->nbhid', scores, vc)
    q_in = qc * jnp.exp(b)
    k_up = kc * jnp.exp(b[..., -1:, :] - b)
    a_last = jnp.exp(b[..., -1, :])

    def step(state, inp):
        q_i, k_i, v_i, a_i = inp
        o_i = jnp.einsum('bhic,bhcd->bhid', q_i, state)
        state = a_i[..., None] * state + jnp.einsum('bhic,bhid->bhcd', k_i, v_i)
        return state, o_i

    _, o_inter = lax.scan(step, jnp.zeros((B, H, dk, dv), jnp.float32), (q_in, k_up, vc, a_last))
    return _unchunk(o_intra + o_inter).astype(v.dtype)


def gla_branch(q, k, v, lr, r, w_lr2, b_lr, norm_g):
    B, S, H, dk = q.shape
    log_a = jax.nn.log_sigmoid((lr @ w_lr2 + b_lr).astype(jnp.float32)) / GLA_GATE_NORM
    o = gla_chunked(q * dk ** -0.5, k, v, log_a.reshape(B, S, H, dk))
    return rms_norm(o, norm_g) * jax.nn.silu(r)


def gated_delta_chunked(q, k, v, g, beta):
    B, S, H, dk = q.shape
    dv = v.shape[-1]
    c = DN_CHUNK
    qc, kc, vc = _chunk(q, c), _chunk(k, c), _chunk(v, c)
    bc = _chunk(beta, c)
    G = jnp.cumsum(_chunk(g, c), axis=-1)
    incl = jnp.tril(jnp.ones((c, c), dtype=bool))
    strict = jnp.tril(jnp.ones((c, c), dtype=bool), -1)
    decay = jnp.exp(jnp.where(incl, G[..., :, None] - G[..., None, :], -jnp.inf))
    kb = kc * bc[..., None]
    lower = jnp.where(strict, jnp.einsum('nbhic,nbhjc->nbhij', kb, kc) * decay, 0.0)
    rhs = jnp.concatenate([vc * bc[..., None], kb * jnp.exp(G)[..., None]], axis=-1)
    sol = lax.linalg.triangular_solve(jnp.eye(c, dtype=jnp.float32) + lower, rhs,
                                      left_side=True, lower=True, unit_diagonal=True)
    u, w = sol[..., :dv], sol[..., dv:]
    scores = jnp.where(incl, jnp.einsum('nbhic,nbhjc->nbhij', qc, kc) * decay, 0.0)
    q_in = qc * jnp.exp(G)[..., None]
    k_up = kc * jnp.exp(G[..., -1:] - G)[..., None]
    a_last = jnp.exp(G[..., -1])

    def step(state, inp):
        q_i, k_i, u_i, w_i, s_i, a_i = inp
        v_new = u_i - jnp.einsum('bhic,bhcd->bhid', w_i, state)
        o_i = jnp.einsum('bhic,bhcd->bhid', q_i, state) + jnp.einsum('bhij,bhjd->bhid', s_i, v_new)
        state = a_i[..., None, None] * state + jnp.einsum('bhic,bhid->bhcd', k_i, v_new)
        return state, o_i

    _, o = lax.scan(step, jnp.zeros((B, H, dk, dv), jnp.float32), (q_in, k_up, u, w, scores, a_last))
    return _unchunk(o).astype(v.dtype)


def deltanet_branch(qkv, a, b, gate, conv_w, a_log, dt_bias, norm_g):
    B, S, _ = qkv.shape
    qkv = jax.nn.silu(causal_conv(qkv, conv_w))
    q, k, v = [t.reshape(B, S, DN_HEADS, HEAD_DIM) for t in jnp.split(qkv, 3, axis=-1)]
    q = l2norm(q) * HEAD_DIM ** -0.5
    k = l2norm(k)
    beta = jax.nn.sigmoid(b.astype(jnp.float32))
    g = -jnp.exp(a_log.astype(jnp.float32)) * jax.nn.softplus(a.astype(jnp.float32) + dt_bias.astype(jnp.float32))
    o = gated_delta_chunked(q, k, v, g, beta)
    return rms_norm(o, norm_g) * jax.nn.silu(gate)


def dilated_pattern(q, k, v, dil, n_keys):
    B, S, H, dh = q.shape
    L = S // dil
    nb = -(-L // DIL_BLOCK)
    Lp = nb * DIL_BLOCK

    def strided(t):
        t = t.reshape(B, L, dil, H, dh).transpose(0, 2, 3, 1, 4)
        t = jnp.pad(t, ((0, 0), (0, 0), (0, 0), (0, Lp - L), (0, 0)))
        return t.reshape(B, dil, H, nb, DIL_BLOCK, dh)

    def with_prev(t):
        prev = jnp.pad(t[:, :, :, :-1], ((0, 0), (0, 0), (0, 0), (1, 0), (0, 0), (0, 0)))
        return jnp.concatenate([prev, t], axis=-2)

    qs = strided(q)
    kk = with_prev(strided(k))
    vv = with_prev(strided(v)).astype(jnp.float32)
    s = jnp.einsum('brhnqc,brhnkc->brhnqk', qs, kk).astype(jnp.float32) * HEAD_DIM ** -0.5
    qi = jnp.arange(DIL_BLOCK)[:, None]
    kj = jnp.arange(2 * DIL_BLOCK)[None, :]
    m = qi + DIL_BLOCK - kj
    key_n = jnp.arange(nb)[:, None, None] * DIL_BLOCK - DIL_BLOCK + kj[None]
    valid = (m >= 0) & (m <= n_keys) & (key_n >= 0)
    s = jnp.where(valid, s, -jnp.inf)
    mx = jnp.max(s, axis=-1)
    p = jnp.exp(s - mx[..., None])
    den = jnp.sum(p, axis=-1)
    num = jnp.einsum('brhnqk,brhnkc->brhnqc', p, vv)

    def unstrided(t):
        t = t.reshape(B, dil, H, Lp, *t.shape[5:])[:, :, :, :L]
        t = jnp.moveaxis(t, 3, 1)
        return t.reshape(B, S, H, *t.shape[4:])

    return unstrided(mx), unstrided(den), unstrided(num)


def dilated_branch(q, k, v, q_g, k_g):
    S = q.shape[1]
    pos = jnp.arange(S, dtype=jnp.int32)
    q = rope_partial(rms_norm(q, q_g), pos)
    k = rope_partial(rms_norm(k, k_g), pos)
    stats = [dilated_pattern(q, k, v, dil, window // dil) for window, dil in DIL_PATTERNS]
    mx = jnp.stack([st[0] for st in stats])
    den = jnp.stack([st[1] for st in stats])
    num = jnp.stack([st[2] for st in stats])
    w = jnp.exp(mx - jnp.max(mx, axis=0, keepdims=True))
    out = jnp.sum(w[..., None] * num, axis=0) / jnp.sum(w * den, axis=0)[..., None]
    return out.astype(v.dtype)


def hybrid_layer(x, g_mix, g_mlp, w_in, gla_w_lr2, gla_b_lr, gla_norm_g, dn_conv_w, dn_a_log, dn_dt_bias,
                 dn_norm_g, dil_q_g, dil_k_g, w_branch, w_out, w_mlp_in, w_mlp_out):
    B, S, _ = x.shape
    h = rms_norm(x, g_mix)
    (sb_q, sb_k, sb_v, gla_q, gla_k, gla_v, gla_lr, gla_r, dn_qkv, dn_a, dn_b, dn_gate,
     dil_q, dil_k, dil_v, gate_logits) = jnp.split(h @ w_in, IN_OFFSETS, axis=-1)

    def heads(t, n):
        return t.reshape(B, S, n, -1)

    o_sb = stick_breaking_attention(heads(sb_q, SB_HEADS), heads(sb_k, SB_HEADS), heads(sb_v, SB_HEADS))
    o_gla = gla_branch(heads(gla_q, GLA_HEADS), heads(gla_k, GLA_HEADS), heads(gla_v, GLA_HEADS), gla_lr,
                       heads(gla_r, GLA_HEADS), gla_w_lr2, gla_b_lr, gla_norm_g)
    o_dn = deltanet_branch(dn_qkv, dn_a, dn_b, heads(dn_gate, DN_HEADS), dn_conv_w, dn_a_log, dn_dt_bias, dn_norm_g)
    o_dil = dilated_branch(heads(dil_q, DIL_HEADS), heads(dil_k, DIL_HEADS), heads(dil_v, DIL_HEADS), dil_q_g, dil_k_g)

    gates = jax.nn.sigmoid(gate_logits.reshape(B, S, N_BRANCH, D_MODEL))
    merged = jnp.zeros_like(x)
    for n, o in enumerate((o_sb, o_gla, o_dn, o_dil)):
        merged = merged + gates[:, :, n] * (o.reshape(B, S, BRANCH_WIDTH) @ w_branch[n])
    x = x + merged @ w_out
    h = rms_norm(x, g_mlp)
    return x + jnp.square(jax.nn.relu(h @ w_mlp_in)) @ w_mlp_out


def setup_inputs(seed: int = 0) -> dict:
    key = jax.random.key(seed)
    ks = jax.random.split(key, 18)
    f32 = jnp.float32

    def nrm(k, shape, scale):
        return scale * jax.random.normal(k, shape, f32)

    dt = jnp.exp(jax.random.uniform(ks[8], (DEPTH, DN_HEADS), f32, math.log(1e-3), math.log(1e-1)))
    return {
        'x': jax.random.normal(ks[0], (BATCH, SEQ, D_MODEL), f32),
        'norm_mix_g': 1.0 + nrm(ks[1], (DEPTH, D_MODEL), 0.02),
        'norm_mlp_g': 1.0 + nrm(ks[2], (DEPTH, D_MODEL), 0.02),
        'w_in': nrm(ks[3], (DEPTH, D_MODEL, N_IN), D_MODEL ** -0.5),
        'gla_w_lr2': nrm(ks[4], (DEPTH, GLA_LOW_RANK, GLA_HEADS * GLA_KEY_DIM), GLA_LOW_RANK ** -0.5),
        'gla_b_lr': nrm(ks[5], (DEPTH, GLA_HEADS * GLA_KEY_DIM), 0.1),
        'gla_norm_g': 1.0 + nrm(ks[6], (DEPTH, GLA_VAL_DIM), 0.02),
        'dn_conv_w': nrm(ks[7], (DEPTH, 3 * DN_HEADS * HEAD_DIM, DN_CONV), DN_CONV ** -0.5),
        'dn_a_log': jnp.log(jax.random.uniform(ks[9], (DEPTH, DN_HEADS), f32, 1.0, 16.0)),
        'dn_dt_bias': dt + jnp.log(-jnp.expm1(-dt)),
        'dn_norm_g': 1.0 + nrm(ks[10], (DEPTH, HEAD_DIM), 0.02),
        'dil_q_norm_g': 1.0 + nrm(ks[11], (DEPTH, HEAD_DIM), 0.02),
        'dil_k_norm_g': 1.0 + nrm(ks[12], (DEPTH, HEAD_DIM), 0.02),
        'w_branch': nrm(ks[13], (DEPTH, N_BRANCH, BRANCH_WIDTH, D_MODEL), BRANCH_WIDTH ** -0.5),
        'w_out': nrm(ks[14], (DEPTH, D_MODEL, D_MODEL), D_MODEL ** -0.5),
        'w_mlp_in': nrm(ks[15], (DEPTH, D_MODEL, D_FF), D_MODEL ** -0.5),
        'w_mlp_out': nrm(ks[16], (DEPTH, D_FF, D_MODEL), 0.5 * D_FF ** -0.5),
    }


def reference(x, norm_mix_g, norm_mlp_g, w_in, gla_w_lr2, gla_b_lr, gla_norm_g, dn_conv_w, dn_a_log, dn_dt_bias,
              dn_norm_g, dil_q_norm_g, dil_k_norm_g, w_branch, w_out, w_mlp_in, w_mlp_out):
    for l in range(DEPTH):
        x = hybrid_layer(x, norm_mix_g[l], norm_mlp_g[l], w_in[l], gla_w_lr2[l], gla_b_lr[l], gla_norm_g[l],
                         dn_conv_w[l], dn_a_log[l], dn_dt_bias[l], dn_norm_g[l], dil_q_norm_g[l], dil_k_norm_g[l],
                         w_branch[l], w_out[l], w_mlp_in[l], w_mlp_out[l])
    return x
```

```python
import functools

import jax
import jax.numpy as jnp
from jax import lax
from jax.experimental import pallas as pl
from jax.experimental.pallas import tpu as pltpu

F32 = jnp.float32
BF16 = jnp.bfloat16

HEAD_DIM = 64
N_HEADS = 4
BRANCH_WIDTH = N_HEADS * HEAD_DIM
GLA_KEY_DIM = 32
GLA_LOW_RANK = 16
GLA_GATE_NORM = 16.0
GLA_KEY_WIDTH = N_HEADS * GLA_KEY_DIM
DN_CONV = 4
DIL_DILATIONS = (1, 4, 16)
DIL_KEYS = 128
ROPE_THETA = 500000.0
ROT_DIM = HEAD_DIM // 4
N_BRANCH = 4
EPS = 1e-6

SB_W = 3 * BRANCH_WIDTH
GLA_W = 896
DN_W = 1152
DIL_W = 3 * BRANCH_WIDTH
MIX_W = SB_W + GLA_W + DN_W + DIL_W

ROW_TILE = 512
SB_TILE = 256
GLA_CHUNK = 64
GLA_SUB = 16
GLA_STEP = 256
DN_CHUNK = 64
DIL_BLOCK = 128
DIL_SUPER = DIL_BLOCK * DIL_DILATIONS[-1]
FF_TILE = 1024

V7X_VMEM_LIMIT = 56 * 1024 * 1024


def _dot(a, b):
    return jnp.dot(a, b, preferred_element_type=F32)


def _dot_nt(a, b):
    return lax.dot_general(a, b, (((1,), (1,)), ((), ())), preferred_element_type=F32)


def _dot_tn(a, b):
    return lax.dot_general(a, b, (((0,), (0,)), ((), ())), preferred_element_type=F32)


def _split2(a):
    hi = a.astype(BF16)
    lo = (a - hi.astype(F32)).astype(BF16)
    return hi, lo


def _split3(a):
    hi = a.astype(BF16)
    r = a - hi.astype(F32)
    mid = r.astype(BF16)
    lo = (r - mid.astype(F32)).astype(BF16)
    return hi, mid, lo


def _dot3(a, b, dot=_dot):
    ah, al = _split2(a)
    bh, bl = _split2(b)
    return dot(ah, bh) + (dot(ah, bl) + dot(al, bh))


def _dot_exact_rhs(a, b_exact, dot=_dot, parts=2):
    pieces = _split2(a) if parts == 2 else _split3(a)
    out = dot(pieces[0], b_exact)
    for p in pieces[1:]:
        out = out + dot(p, b_exact)
    return out


def _iota(shape, dim):
    return lax.broadcasted_iota(jnp.int32, shape, dim)


def _group_ones(rows, cols, rgroup, cgroup):
    return (_iota((rows, cols), 0) // rgroup == _iota((rows, cols), 1) // cgroup).astype(BF16)


def _log_sigmoid(z):
    return jnp.minimum(z, 0.0) - jnp.log1p(jnp.exp(-jnp.abs(z)))


def _softplus(z):
    return jnp.maximum(z, 0.0) + jnp.log1p(jnp.exp(-jnp.abs(z)))


def _silu(z):
    return z * jax.nn.sigmoid(z)


def _head_rms(x, gain_row, head_ones):
    ms = _dot_exact_rhs(x * x, head_ones) * (1.0 / HEAD_DIM)
    return x * lax.rsqrt(ms + EPS) * gain_row


def _params(*semantics):
    return pltpu.CompilerParams(dimension_semantics=semantics, vmem_limit_bytes=V7X_VMEM_LIMIT)


def _resident(shape):
    nd = len(shape)
    return pl.BlockSpec(shape, lambda *_: (0,) * nd, pipeline_mode=pl.Buffered(1))


def _in_proj_body(x_ref, g_ref, w_ref, cos_ref, sin_ref, qg_ref, kg_ref, sb_ref, gla_ref, dn_ref, dil_ref):
    x = x_ref[...]
    h = (x * lax.rsqrt(jnp.mean(x * x, axis=-1, keepdims=True) + EPS) * g_ref[...]).astype(BF16)
    sb_ref[...] = _dot(h, w_ref[:, 0:SB_W]).astype(BF16)
    off = SB_W
    gla_ref[...] = _dot(h, w_ref[:, off:off + GLA_W])
    off += GLA_W
    dn_ref[...] = _dot(h, w_ref[:, off:off + DN_W])
    off += DN_W
    d = _dot(h, w_ref[:, off:off + DIL_W])

    head_ones = _group_ones(BRANCH_WIDTH, BRANCH_WIDTH, HEAD_DIM, HEAD_DIM)
    lane = _iota((1, BRANCH_WIDTH), 1) % HEAD_DIM
    first_half = lane < ROT_DIM // 2
    cos = cos_ref[...]
    sin = sin_ref[...]

    def norm_rope(t, gain_row):
        y = _head_rms(t, gain_row, head_ones)
        partner = jnp.where(first_half, pltpu.roll(y, BRANCH_WIDTH - ROT_DIM // 2, 1), pltpu.roll(y, ROT_DIM // 2, 1))
        return y * cos + partner * sin

    dil_ref[:, 0:BRANCH_WIDTH] = norm_rope(d[:, 0:BRANCH_WIDTH], qg_ref[...]) * (HEAD_DIM ** -0.5)
    dil_ref[:, BRANCH_WIDTH:2 * BRANCH_WIDTH] = norm_rope(d[:, BRANCH_WIDTH:2 * BRANCH_WIDTH], kg_ref[...])
    dil_ref[:, 2 * BRANCH_WIDTH:] = d[:, 2 * BRANCH_WIDTH:]


def _in_proj(x2, g_row, w_mix, cos_t, sin_t, qg_row, kg_row, seq):
    t, d_model = x2.shape
    tm = ROW_TILE
    n_seq_tiles = seq // tm
    row = lambda w: pl.BlockSpec((tm, w), lambda i: (i, 0))
    tab = pl.BlockSpec((tm, BRANCH_WIDTH), lambda i: (i % n_seq_tiles, 0))
    return pl.pallas_call(
        _in_proj_body,
        grid=(t // tm,),
        in_specs=[row(d_model), _resident((1, d_model)), _resident((d_model, MIX_W)), tab, tab,
                  _resident((1, BRANCH_WIDTH)), _resident((1, BRANCH_WIDTH))],
        out_specs=[row(SB_W), row(GLA_W), row(DN_W), row(DIL_W)],
        out_shape=[jax.ShapeDtypeStruct((t, SB_W), BF16), jax.ShapeDtypeStruct((t, GLA_W), F32),
                   jax.ShapeDtypeStruct((t, DN_W), F32), jax.ShapeDtypeStruct((t, DIL_W), F32)],
        compiler_params=_params("parallel"),
        name="in_proj",
    )(x2, g_row, w_mix, cos_t, sin_t, qg_row, kg_row)


def _sb_body(q_ref, k_ref, v_ref, o_ref):
    tq = q_ref.shape[0]
    qi = pl.program_id(1)
    q = q_ref[...]
    head_of_lane = _iota((1, BRANCH_WIDTH), 1) // HEAD_DIM
    q_heads = [jnp.where(head_of_lane == h, q, jnp.zeros_like(q)) for h in range(N_HEADS)]
    rows = _iota((tq, tq), 0)
    cols = _iota((tq, tq), 1)
    later = (rows > cols).astype(BF16)
    causal = cols < rows

    def block(j, masked, acc, run):
        kb = k_ref[pl.ds(pl.multiple_of(j * tq, tq), tq), :]
        vb = v_ref[pl.ds(pl.multiple_of(j * tq, tq), tq), :]
        new_run = []
        pv = None
        for h in range(N_HEADS):
            z = _dot_nt(q_heads[h], kb)
            lg = jnp.log1p(jnp.exp(-jnp.abs(z)))
            sp = jnp.maximum(z, 0.0) + lg
            log_beta = jnp.minimum(z, 0.0) - lg
            if masked:
                sp = jnp.where(causal, sp, 0.0)
            after = _dot_exact_rhs(sp, later)
            w = jnp.exp(log_beta - after - run[h])
            if masked:
                w = jnp.where(causal, w, 0.0)
            pv_h = _dot(w.astype(BF16), vb)
            pv = pv_h if pv is None else jnp.where(head_of_lane == h, pv_h, pv)
            new_run.append(run[h] + (after[:, 0:1] + sp[:, 0:1]))
        return acc + pv, tuple(new_run)

    acc0 = jnp.zeros((tq, BRANCH_WIDTH), F32)
    run0 = tuple(jnp.zeros((tq, 1), F32) for _ in range(N_HEADS))
    acc, run = block(qi, True, acc0, run0)

    def body(step, carry):
        acc, run = carry
        return block(qi - 1 - step, False, acc, run)

    acc, run = lax.fori_loop(0, qi, body, (acc, run))
    o_ref[...] = acc.astype(o_ref.dtype)


def _stick_breaking(sb3):
    b, s, _ = sb3.shape
    tq = SB_TILE
    whole = lambda blk: pl.BlockSpec((None, s, BRANCH_WIDTH), lambda bi, i: (bi, 0, blk), pipeline_mode=pl.Buffered(1))
    return pl.pallas_call(
        _sb_body,
        grid=(b, s // tq),
        in_specs=[pl.BlockSpec((None, tq, BRANCH_WIDTH), lambda bi, i: (bi, i, 0)), whole(1), whole(2)],
        out_specs=pl.BlockSpec((None, tq, BRANCH_WIDTH), lambda bi, i: (bi, i, 0)),
        out_shape=jax.ShapeDtypeStruct((b, s, BRANCH_WIDTH), BF16),
        compiler_params=_params("parallel", "arbitrary"),
        name="stick_breaking",
    )(sb3, sb3, sb3)


def _gla_body(x_ref, w2_ref, b2_ref, g_ref, o_ref, state_ref):
    c = GLA_CHUNK
    nsub = c // GLA_SUB

    @pl.when(pl.program_id(1) == 0)
    def _():
        state_ref[...] = jnp.zeros_like(state_ref)

    ri = _iota((c, c), 0)
    ci = _iota((c, c), 1)
    tril_incl = (ci <= ri).astype(BF16)
    earlier_sub = (ci // GLA_SUB) < (ri // GLA_SUB)
    key_head = _iota((1, GLA_KEY_WIDTH), 1) // GLA_KEY_DIM
    val_head = _iota((1, BRANCH_WIDTH), 1) // HEAD_DIM
    key_to_val = _group_ones(GLA_KEY_WIDTH, BRANCH_WIDTH, GLA_KEY_DIM, HEAD_DIM)
    state_mask = key_to_val > 0
    head_ones = _group_ones(BRANCH_WIDTH, BRANCH_WIDTH, HEAD_DIM, HEAD_DIM)
    ones_cv = jnp.ones((c, BRANCH_WIDTH), BF16)
    sub_pos = _iota((1, GLA_SUB, 1), 1)
    scale = GLA_KEY_DIM ** -0.5

    def chunk(ic, carry):
        r0 = pl.multiple_of(ic * c, c)
        q = x_ref[pl.ds(r0, c), 0:128] * scale
        k = x_ref[pl.ds(r0, c), 128:256]
        v = x_ref[pl.ds(r0, c), 256:512]
        gate_r = x_ref[pl.ds(r0, c), 512:768]
        lr = x_ref[pl.ds(r0, c), 768:896]
        zg = jnp.dot(lr, w2_ref[...], precision=lax.Precision.HIGHEST, preferred_element_type=F32) + b2_ref[...]
        la = _log_sigmoid(zg) * (1.0 / GLA_GATE_NORM)
        bcum = _dot_exact_rhs(la, tril_incl, dot=lambda p, m: _dot(m, p), parts=3)
        btot_kv = _dot_exact_rhs(la, ones_cv, dot=_dot_tn, parts=3)
        b_last = bcum[c - 1:c, :]
        state = state_ref[...]

        qe = q * jnp.exp(bcum)
        ke = k * jnp.exp(-bcum)
        kl = k * jnp.exp(b_last - bcum)
        qe_b = qe.astype(BF16)
        ke_b = ke.astype(BF16)

        o = _dot(qe_b, state.astype(BF16))

        p_heads = []
        v_heads = []
        for h in range(N_HEADS):
            sc = _dot_nt(jnp.where(key_head == h, qe_b, jnp.zeros_like(qe_b)), ke_b)
            p_heads.append(jnp.where(earlier_sub, sc, 0.0).astype(BF16))
            v_heads.append(jnp.where(val_head == h, v, 0.0).astype(BF16))
        o = o + _dot(jnp.concatenate(p_heads, axis=1), jnp.concatenate(v_heads, axis=0))

        q3 = q.reshape(nsub, GLA_SUB, GLA_KEY_WIDTH)
        k3 = k.reshape(nsub, GLA_SUB, GLA_KEY_WIDTH)
        b3 = bcum.reshape(nsub, GLA_SUB, GLA_KEY_WIDTH)
        v3 = v.reshape(nsub, GLA_SUB, BRANCH_WIDTH)
        for jj in range(GLA_SUB):
            kj = k3[:, jj:jj + 1, :]
            bj = b3[:, jj:jj + 1, :]
            vj = jnp.broadcast_to(v3[:, jj:jj + 1, :], v3.shape).reshape(c, BRANCH_WIDTH)
            dec = jnp.exp(jnp.where(sub_pos >= jj, b3 - bj, -jnp.inf))
            pj = (q3 * kj * dec).reshape(c, GLA_KEY_WIDTH).astype(BF16)
            o = o + _dot(pj, key_to_val) * vj

        state_ref[...] = state * jnp.exp(btot_kv) + jnp.where(state_mask, _dot_tn(kl.astype(BF16), v.astype(BF16)), 0.0)
        y = _head_rms(o, g_ref[...], head_ones) * _silu(gate_r)
        o_ref[pl.ds(r0, c), :] = y.astype(o_ref.dtype)
        return carry

    lax.fori_loop(0, GLA_STEP // c, chunk, 0)


def _gla(gla3, w2_pad, b2_row, g_row):
    b, s, _ = gla3.shape
    return pl.pallas_call(
        _gla_body,
        grid=(b, s // GLA_STEP),
        in_specs=[pl.BlockSpec((None, GLA_STEP, GLA_W), lambda bi, i: (bi, i, 0)),
                  _resident((GLA_KEY_WIDTH, GLA_KEY_WIDTH)), _resident((1, GLA_KEY_WIDTH)),
                  _resident((1, BRANCH_WIDTH))],
        out_specs=pl.BlockSpec((None, GLA_STEP, BRANCH_WIDTH), lambda bi, i: (bi, i, 0)),
        out_shape=jax.ShapeDtypeStruct((b, s, BRANCH_WIDTH), BF16),
        scratch_shapes=[pltpu.VMEM((GLA_KEY_WIDTH, BRANCH_WIDTH), F32)],
        compiler_params=_params("parallel", "arbitrary"),
        name="gla",
    )(gla3, w2_pad, b2_row, g_row)


def _dn_body(x_ref, cw_ref, a_ref, dt_ref, g_ref, o_ref, xbuf_ref, state_ref):
    c = DN_CHUNK
    qkv_w = 3 * BRANCH_WIDTH

    @pl.when(pl.program_id(1) == 0)
    def _():
        state_ref[...] = jnp.zeros_like(state_ref)
        xbuf_ref[0:8, :] = jnp.zeros((8, qkv_w), F32)

    xbuf_ref[8:8 + c, :] = x_ref[:, 0:qkv_w]
    conv = cw_ref[DN_CONV - 1:DN_CONV, :] * xbuf_ref[8:8 + c, :]
    for i in range(DN_CONV - 1):
        conv = conv + cw_ref[i:i + 1, :] * xbuf_ref[pl.ds(8 - (DN_CONV - 1) + i, c), :]
    xbuf_ref[0:8, :] = xbuf_ref[c:c + 8, :]
    act = _silu(conv)

    head_ones = _group_ones(BRANCH_WIDTH, BRANCH_WIDTH, HEAD_DIM, HEAD_DIM)

    def l2n(t):
        return t * lax.rsqrt(_dot_exact_rhs(t * t, head_ones) + EPS)

    q_all = l2n(act[:, 0:BRANCH_WIDTH]) * (HEAD_DIM ** -0.5)
    k_all = l2n(act[:, BRANCH_WIDTH:2 * BRANCH_WIDTH])
    v_all = act[:, 2 * BRANCH_WIDTH:]

    ab = x_ref[:, qkv_w + BRANCH_WIDTH:]
    g_tok = a_ref[...] * _softplus(ab + dt_ref[...])
    beta_all = jax.nn.sigmoid(ab)

    ri = _iota((c, c), 0)
    ci = _iota((c, c), 1)
    incl = ci <= ri
    strict = ci < ri
    g_col = _dot_exact_rhs(g_tok, incl.astype(BF16), dot=lambda p, m: _dot(m, p), parts=3)
    g_row = _dot_exact_rhs(g_tok, (ri <= ci).astype(BF16), dot=_dot_tn, parts=3)
    eye = (ri == ci).astype(F32)
    level_masks = []
    s = 1
    while s < c:
        level_masks.append((ri // (2 * s) == ci // (2 * s)) & (ri % (2 * s) >= s) & (ci % (2 * s) < s))
        s *= 2

    outs = []
    for h in range(N_HEADS):
        sl = slice(h * HEAD_DIM, (h + 1) * HEAD_DIM)
        q, k, v = q_all[:, sl], k_all[:, sl], v_all[:, sl]
        beta = beta_all[:, N_HEADS + h:N_HEADS + h + 1]
        gc = g_col[:, h:h + 1]
        gr = g_row[h:h + 1, :]
        g_last = g_col[c - 1:c, h:h + 1]
        decay = jnp.exp(jnp.where(incl, gc - gr, -jnp.inf))
        kb = k * beta
        lower = jnp.where(strict, _dot3(kb, k, dot=_dot_nt) * decay, 0.0)
        tinv = eye - jnp.where(level_masks[0], lower, 0.0)
        for m in level_masks[1:]:
            tinv = tinv - _dot3(_dot3(tinv, jnp.where(m, lower, 0.0)), tinv)
        egc = jnp.exp(gc)
        sol = _dot3(tinv, jnp.concatenate([v * beta, kb * egc], axis=1))
        u, w = sol[:, 0:HEAD_DIM], sol[:, HEAD_DIM:]
        scores = jnp.where(incl, _dot_nt(q.astype(BF16), k.astype(BF16)) * decay, 0.0)
        q_in = q * egc
        k_up = k * jnp.exp(g_last - gc)
        state = state_ref[h]
        state_b = state.astype(BF16)
        v_new = u - _dot(w.astype(BF16), state_b)
        o_h = _dot(q_in.astype(BF16), state_b) + _dot(scores.astype(BF16), v_new.astype(BF16))
        state_ref[h] = jnp.exp(g_last) * state + _dot_tn(k_up.astype(BF16), v_new.astype(BF16))
        outs.append(o_h)

    o = jnp.concatenate(outs, axis=1)
    gate = x_ref[:, qkv_w:qkv_w + BRANCH_WIDTH]
    o_ref[...] = (_head_rms(o, g_ref[...], head_ones) * _silu(gate)).astype(o_ref.dtype)


def _deltanet(dn3, conv_w_t, a_row, dt_row, g_row):
    b, s, _ = dn3.shape
    c = DN_CHUNK
    return pl.pallas_call(
        _dn_body,
        grid=(b, s // c),
        in_specs=[pl.BlockSpec((None, c, DN_W), lambda bi, i: (bi, i, 0)),
                  _resident((DN_CONV, 3 * BRANCH_WIDTH)), _resident((1, 128)), _resident((1, 128)),
                  _resident((1, BRANCH_WIDTH))],
        out_specs=pl.BlockSpec((None, c, BRANCH_WIDTH), lambda bi, i: (bi, i, 0)),
        out_shape=jax.ShapeDtypeStruct((b, s, BRANCH_WIDTH), BF16),
        scratch_shapes=[pltpu.VMEM((c + 8, 3 * BRANCH_WIDTH), F32), pltpu.VMEM((N_HEADS, HEAD_DIM, HEAD_DIM), F32)],
        compiler_params=_params("parallel", "arbitrary"),
        name="deltanet",
    )(dn3, conv_w_t, a_row, dt_row, g_row)


def _ld_halves(ref, rows):
    return jnp.concatenate([ref[0, rows, :], ref[1, rows, :]], axis=1)


def _st_halves(ref, rows, val):
    ref[0, rows, :] = val[:, 0:128]
    ref[1, rows, :] = val[:, 128:256]


def _dil_body(q_ref, kc_ref, kp_ref, vc_ref, vp_ref, o_ref, qx_ref, kx_ref, vx_ref, m_ref, l_ref, acc_ref):
    n = pl.program_id(1)
    sup = DIL_SUPER
    blk = DIL_BLOCK
    _st_halves(qx_ref, slice(0, sup), q_ref[...])
    _st_halves(kx_ref, slice(0, sup), kp_ref[...])
    _st_halves(kx_ref, slice(sup, 2 * sup), kc_ref[...])
    _st_halves(vx_ref, slice(0, sup), vp_ref[...])
    _st_halves(vx_ref, slice(sup, 2 * sup), vc_ref[...])

    head_of_lane = _iota((1, BRANCH_WIDTH), 1) // HEAD_DIM
    qi = _iota((blk, 2 * blk), 0)
    kj = _iota((blk, 2 * blk), 1)
    dist = qi + blk - kj
    band = (dist >= 0) & (dist <= DIL_KEYS)
    in_current = kj >= blk

    for p, dil in enumerate(DIL_DILATIONS):
        def qblock(qb, carry, p=p, dil=dil):
            r = qb % dil
            m = qb // dil
            q_start = r + dil * blk * m
            k_start = sup + r + dil * blk * (m - 1)
            if dil == 1:
                q_rows = pl.ds(q_start, blk)
                k_rows = pl.ds(k_start, 2 * blk)
            else:
                q_rows = pl.ds(q_start, blk, stride=dil)
                k_rows = pl.ds(k_start, 2 * blk, stride=dil)
            q = _ld_halves(qx_ref, q_rows).astype(BF16)
            k = _ld_halves(kx_ref, k_rows).astype(BF16)
            v = _ld_halves(vx_ref, k_rows).astype(BF16)
            has_prev = jnp.logical_or(n > 0, m > 0)
            valid = band & jnp.logical_or(in_current, has_prev)
            mx_t = den_t = num = None
            for h in range(N_HEADS):
                s = _dot_nt(jnp.where(head_of_lane == h, q, jnp.zeros_like(q)), k)
                s = jnp.where(valid, s, -jnp.inf)
                mx = jnp.max(s, axis=-1, keepdims=True)
                pr = jnp.exp(s - mx)
                den = jnp.sum(pr, axis=-1, keepdims=True)
                num_h = _dot(pr.astype(BF16), v)
                sel = head_of_lane == h
                if h == 0:
                    mx_t = jnp.broadcast_to(mx, (blk, BRANCH_WIDTH))
                    den_t = jnp.broadcast_to(den, (blk, BRANCH_WIDTH))
                    num = num_h
                else:
                    mx_t = jnp.where(sel, mx, mx_t)
                    den_t = jnp.where(sel, den, den_t)
                    num = jnp.where(sel, num_h, num)
            if p == 0:
                _st_halves(m_ref, q_rows, mx_t)
                _st_halves(l_ref, q_rows, den_t)
                _st_halves(acc_ref, q_rows, num)
            else:
                m_old = _ld_halves(m_ref, q_rows)
                m_new = jnp.maximum(m_old, mx_t)
                a_old = jnp.exp(m_old - m_new)
                a_new = jnp.exp(mx_t - m_new)
                _st_halves(m_ref, q_rows, m_new)
                _st_halves(l_ref, q_rows, a_old * _ld_halves(l_ref, q_rows) + a_new * den_t)
                _st_halves(acc_ref, q_rows, a_old * _ld_halves(acc_ref, q_rows) + a_new * num)
            return carry

        lax.fori_loop(0, sup // blk, qblock, 0)

    whole = slice(0, sup)
    o_ref[...] = (_ld_halves(acc_ref, whole) / _ld_halves(l_ref, whole)).astype(o_ref.dtype)


def _dilated(dil3):
    b, s, _ = dil3.shape
    sup = DIL_SUPER
    cur = lambda blk: pl.BlockSpec((None, sup, BRANCH_WIDTH), lambda bi, i: (bi, i, blk))
    prev = lambda blk: pl.BlockSpec((None, sup, BRANCH_WIDTH), lambda bi, i: (bi, jnp.maximum(i - 1, 0), blk))
    return pl.pallas_call(
        _dil_body,
        grid=(b, s // sup),
        in_specs=[cur(0), cur(1), prev(1), cur(2), prev(2)],
        out_specs=pl.BlockSpec((None, sup, BRANCH_WIDTH), lambda bi, i: (bi, i, 0)),
        out_shape=jax.ShapeDtypeStruct((b, s, BRANCH_WIDTH), BF16),
        scratch_shapes=[pltpu.VMEM((2, sup, 128), F32), pltpu.VMEM((2, 2 * sup, 128), F32),
                        pltpu.VMEM((2, 2 * sup, 128), F32), pltpu.VMEM((2, sup, 128), F32),
                        pltpu.VMEM((2, sup, 128), F32), pltpu.VMEM((2, sup, 128), F32)],
        compiler_params=_params("parallel", "arbitrary"),
        name="dilated",
    )(dil3, dil3, dil3, dil3, dil3)


def _merge_body(x_ref, g_ref, wg_ref, o0_ref, o1_ref, o2_ref, o3_ref, wb_ref, wo_ref, out_ref):
    x = x_ref[...]
    d_model = x.shape[-1]
    h = (x * lax.rsqrt(jnp.mean(x * x, axis=-1, keepdims=True) + EPS) * g_ref[...]).astype(BF16)
    merged = None
    for nb, o_ref in enumerate((o0_ref, o1_ref, o2_ref, o3_ref)):
        gate = jax.nn.sigmoid(_dot(h, wg_ref[:, nb * d_model:(nb + 1) * d_model]))
        term = gate * _dot(o_ref[...], wb_ref[nb])
        merged = term if merged is None else merged + term
    out_ref[...] = x + _dot(merged.astype(BF16), wo_ref[...])


def _merge(x2, g_row, w_gate, branches, w_branch, w_out):
    t, d_model = x2.shape
    tm = ROW_TILE
    row = lambda w: pl.BlockSpec((tm, w), lambda i: (i, 0))
    return pl.pallas_call(
        _merge_body,
        grid=(t // tm,),
        in_specs=[row(d_model), _resident((1, d_model)), _resident((d_model, N_BRANCH * d_model))]
        + [row(BRANCH_WIDTH)] * N_BRANCH
        + [_resident((N_BRANCH, BRANCH_WIDTH, d_model)), _resident((d_model, d_model))],
        out_specs=row(d_model),
        out_shape=jax.ShapeDtypeStruct((t, d_model), F32),
        compiler_params=_params("parallel"),
        name="merge_out",
    )(x2, g_row, w_gate, *branches, w_branch, w_out)


def _mlp_body(x_ref, g_ref, w1_ref, w2_ref, out_ref):
    x = x_ref[...]
    h = (x * lax.rsqrt(jnp.mean(x * x, axis=-1, keepdims=True) + EPS) * g_ref[...]).astype(BF16)
    acc = x
    for c0 in range(0, w1_ref.shape[1], FF_TILE):
        a = jnp.maximum(_dot(h, w1_ref[:, c0:c0 + FF_TILE]), 0.0)
        acc = acc + _dot((a * a).astype(BF16), w2_ref[c0:c0 + FF_TILE, :])
    out_ref[...] = acc


def _mlp(x2, g_row, w1, w2):
    t, d_model = x2.shape
    d_ff = w1.shape[1]
    tm = ROW_TILE
    row = pl.BlockSpec((tm, d_model), lambda i: (i, 0))
    return pl.pallas_call(
        _mlp_body,
        grid=(t // tm,),
        in_specs=[row, _resident((1, d_model)), _resident((d_model, d_ff)), _resident((d_ff, d_model))],
        out_specs=row,
        out_shape=jax.ShapeDtypeStruct((t, d_model), F32),
        compiler_params=_params("parallel"),
        name="mlp",
    )(x2, g_row, w1, w2)


def _mixer_weights(w_in):
    d_model = w_in.shape[0]
    bw, kw = BRANCH_WIDTH, GLA_KEY_WIDTH
    o = 0
    sb = w_in[:, o:o + 3 * bw]
    o += 3 * bw
    gla_q, gla_k = w_in[:, o:o + kw], w_in[:, o + kw:o + 2 * kw]
    o += 2 * kw
    gla_v = w_in[:, o:o + bw]
    o += bw
    gla_lr = w_in[:, o:o + GLA_LOW_RANK]
    o += GLA_LOW_RANK
    gla_r = w_in[:, o:o + bw]
    o += bw
    dn_qkv = w_in[:, o:o + 3 * bw]
    o += 3 * bw
    dn_ab = w_in[:, o:o + 2 * N_HEADS]
    o += 2 * N_HEADS
    dn_gate = w_in[:, o:o + bw]
    o += bw
    dil = w_in[:, o:o + 3 * bw]
    o += 3 * bw
    zeros = lambda n: jnp.zeros((d_model, n), w_in.dtype)
    sb = jnp.concatenate([sb[:, :bw] * (HEAD_DIM ** -0.5), sb[:, bw:]], axis=1)
    gla = jnp.concatenate([gla_q, gla_k, gla_v, gla_r, gla_lr, zeros(GLA_W - 3 * bw - GLA_LOW_RANK)], axis=1)
    dn = jnp.concatenate([dn_qkv, dn_gate, dn_ab, zeros(DN_W - 4 * bw - 2 * N_HEADS)], axis=1)
    return jnp.concatenate([sb, gla, dn, dil], axis=1).astype(BF16), o


def _rope_tables(seq):
    half = ROT_DIM // 2
    inv_freq = ROPE_THETA ** (-jnp.arange(half, dtype=F32) / half)
    ang = jnp.arange(seq, dtype=jnp.int32).astype(F32)[:, None] * inv_freq[None, :]
    cos, sin = jnp.cos(ang), jnp.sin(ang)
    rest = HEAD_DIM - ROT_DIM
    cos_h = jnp.concatenate([cos, cos, jnp.ones((seq, rest), F32)], axis=1)
    sin_h = jnp.concatenate([-sin, sin, jnp.zeros((seq, rest), F32)], axis=1)
    return jnp.tile(cos_h, (1, N_HEADS)), jnp.tile(sin_h, (1, N_HEADS))


def kernel(x, norm_mix_g, norm_mlp_g, w_in, gla_w_lr2, gla_b_lr, gla_norm_g, dn_conv_w, dn_a_log, dn_dt_bias,
           dn_norm_g, dil_q_norm_g, dil_k_norm_g, w_branch, w_out, w_mlp_in, w_mlp_out):
    b, s, d_model = x.shape
    depth = w_in.shape[0]
    t = b * s
    assert t % ROW_TILE == 0 and s % ROW_TILE == 0 and s % DIL_SUPER == 0 and s % SB_TILE == 0
    cos_t, sin_t = _rope_tables(s)
    tile_heads = lambda g: jnp.tile(g, N_HEADS)[None, :]
    pad_lanes = lambda v: jnp.pad(v, (0, 128 - v.shape[0]))[None, :]

    x2 = x.reshape(t, d_model)
    for l in range(depth):
        w_mix, n_mix = _mixer_weights(w_in[l])
        w_gate = w_in[l][:, n_mix:].astype(BF16)
        sb, gla, dn, dil = _in_proj(x2, norm_mix_g[l][None, :], w_mix, cos_t, sin_t,
                                    tile_heads(dil_q_norm_g[l]), tile_heads(dil_k_norm_g[l]), s)
        o_sb = _stick_breaking(sb.reshape(b, s, SB_W))
        w2_pad = jnp.pad(gla_w_lr2[l], ((0, GLA_KEY_WIDTH - GLA_LOW_RANK), (0, 0)))
        o_gla = _gla(gla.reshape(b, s, GLA_W), w2_pad, gla_b_lr[l][None, :], tile_heads(gla_norm_g[l]))
        o_dn = _deltanet(dn.reshape(b, s, DN_W), dn_conv_w[l].T, pad_lanes(-jnp.exp(dn_a_log[l])),
                         pad_lanes(dn_dt_bias[l]), tile_heads(dn_norm_g[l]))
        o_dil = _dilated(dil.reshape(b, s, DIL_W))
        branches = [o.reshape(t, BRANCH_WIDTH) for o in (o_sb, o_gla, o_dn, o_dil)]
        x2 = _merge(x2, norm_mix_g[l][None, :], w_gate, branches, w_branch[l].astype(BF16), w_out[l].astype(BF16))
        x2 = _mlp(x2, norm_mlp_g[l][None, :], w_mlp_in[l].astype(BF16), w_mlp_out[l].astype(BF16))
    return x2.reshape(b, s, d_model)
```

```python
import functools

import jax
import jax.numpy as jnp
from jax import lax
from jax.experimental import pallas as pl
from jax.experimental.pallas import tpu as pltpu

F32 = jnp.float32
BF16 = jnp.bfloat16

HEAD_DIM = 64
N_HEADS = 4
BRANCH_WIDTH = N_HEADS * HEAD_DIM
GLA_KEY_DIM = 32
GLA_LOW_RANK = 16
GLA_GATE_NORM = 16.0
GLA_KEY_WIDTH = N_HEADS * GLA_KEY_DIM
DN_CONV = 4
DIL_DILATIONS = (1, 4, 16)
DIL_KEYS = 128
ROPE_THETA = 500000.0
ROT_DIM = HEAD_DIM // 4
N_BRANCH = 4
EPS = 1e-6
LOG2_E = 1.4426950408889634

SB_W = 3 * BRANCH_WIDTH
GLA_W = 896
DN_W = 1152
DIL_W = 3 * BRANCH_WIDTH
MIX_W = SB_W + GLA_W + DN_W + DIL_W

ROW_TILE = 512
SB_TILE = 256
GLA_CHUNK = 64
GLA_SUB = 16
GLA_STEP = 256
DN_CHUNK = 64
DIL_BLOCK = 128
DIL_SUPER = DIL_BLOCK * DIL_DILATIONS[-1]
FF_TILE = 1024

V7X_VMEM_LIMIT = 56 * 1024 * 1024


def _dot(a, b):
    return jnp.dot(a, b, preferred_element_type=F32)


def _dot_nt(a, b):
    return lax.dot_general(a, b, (((1,), (1,)), ((), ())), preferred_element_type=F32)


def _dot_tn(a, b):
    return lax.dot_general(a, b, (((0,), (0,)), ((), ())), preferred_element_type=F32)


def _split2(a):
    hi = a.astype(BF16)
    lo = (a - hi.astype(F32)).astype(BF16)
    return hi, lo


def _split3(a):
    hi = a.astype(BF16)
    r = a - hi.astype(F32)
    mid = r.astype(BF16)
    lo = (r - mid.astype(F32)).astype(BF16)
    return hi, mid, lo


def _dot3(a, b, dot=_dot):
    ah, al = _split2(a)
    bh, bl = _split2(b)
    return dot(ah, bh) + (dot(ah, bl) + dot(al, bh))


def _dot_exact_rhs(a, b_exact, dot=_dot, parts=2):
    pieces = _split2(a) if parts == 2 else _split3(a)
    out = dot(pieces[0], b_exact)
    for p in pieces[1:]:
        out = out + dot(p, b_exact)
    return out


def _iota(shape, dim):
    return lax.broadcasted_iota(jnp.int32, shape, dim)


def _group_ones(rows, cols, rgroup, cgroup):
    return (_iota((rows, cols), 0) // rgroup == _iota((rows, cols), 1) // cgroup).astype(BF16)


def _log_sigmoid(z):
    return jnp.minimum(z, 0.0) - jnp.log1p(jnp.exp(-jnp.abs(z)))


def _softplus(z):
    return jnp.maximum(z, 0.0) + jnp.log1p(jnp.exp(-jnp.abs(z)))


def _silu(z):
    return z * jax.nn.sigmoid(z)


def _head_rms(x, gain_row, head_ones):
    ms = _dot_exact_rhs(x * x, head_ones) * (1.0 / HEAD_DIM)
    return x * lax.rsqrt(ms + EPS) * gain_row


def _params(*semantics):
    return pltpu.CompilerParams(dimension_semantics=semantics, vmem_limit_bytes=V7X_VMEM_LIMIT)


def _resident(shape):
    nd = len(shape)
    return pl.BlockSpec(shape, lambda *_: (0,) * nd, pipeline_mode=pl.Buffered(1))


def _in_proj_body(x_ref, g_ref, w_ref, cos_ref, sin_ref, qg_ref, kg_ref, sb_ref, gla_ref, dn_ref, dil_ref):
    x = x_ref[...]
    h = (x * lax.rsqrt(jnp.mean(x * x, axis=-1, keepdims=True) + EPS) * g_ref[...]).astype(BF16)
    q_scale = jnp.where(_iota((1, SB_W), 1) < BRANCH_WIDTH, LOG2_E * HEAD_DIM ** -0.5, 1.0)
    sb_ref[...] = (_dot(h, w_ref[:, 0:SB_W]) * q_scale).astype(BF16)
    off = SB_W
    gla_ref[...] = _dot(h, w_ref[:, off:off + GLA_W])
    off += GLA_W
    dn_ref[...] = _dot(h, w_ref[:, off:off + DN_W])
    off += DN_W
    d = _dot(h, w_ref[:, off:off + DIL_W])

    head_ones = _group_ones(BRANCH_WIDTH, BRANCH_WIDTH, HEAD_DIM, HEAD_DIM)
    lane = _iota((1, BRANCH_WIDTH), 1) % HEAD_DIM
    first_half = lane < ROT_DIM // 2
    cos = cos_ref[...]
    sin = sin_ref[...]

    def norm_rope(t, gain_row):
        y = _head_rms(t, gain_row, head_ones)
        partner = jnp.where(first_half, pltpu.roll(y, BRANCH_WIDTH - ROT_DIM // 2, 1), pltpu.roll(y, ROT_DIM // 2, 1))
        return y * cos + partner * sin

    dil_ref[:, 0:BRANCH_WIDTH] = norm_rope(d[:, 0:BRANCH_WIDTH], qg_ref[...]) * (HEAD_DIM ** -0.5)
    dil_ref[:, BRANCH_WIDTH:2 * BRANCH_WIDTH] = norm_rope(d[:, BRANCH_WIDTH:2 * BRANCH_WIDTH], kg_ref[...])
    dil_ref[:, 2 * BRANCH_WIDTH:] = d[:, 2 * BRANCH_WIDTH:]


def _in_proj(x2, g_row, w_mix, cos_t, sin_t, qg_row, kg_row, seq):
    t, d_model = x2.shape
    tm = ROW_TILE
    n_seq_tiles = seq // tm
    row = lambda w: pl.BlockSpec((tm, w), lambda i: (i, 0))
    tab = pl.BlockSpec((tm, BRANCH_WIDTH), lambda i: (i % n_seq_tiles, 0))
    return pl.pallas_call(
        _in_proj_body,
        grid=(t // tm,),
        in_specs=[row(d_model), _resident((1, d_model)), _resident((d_model, MIX_W)), tab, tab,
                  _resident((1, BRANCH_WIDTH)), _resident((1, BRANCH_WIDTH))],
        out_specs=[row(SB_W), row(GLA_W), row(DN_W), row(DIL_W)],
        out_shape=[jax.ShapeDtypeStruct((t, SB_W), BF16), jax.ShapeDtypeStruct((t, GLA_W), F32),
                   jax.ShapeDtypeStruct((t, DN_W), F32), jax.ShapeDtypeStruct((t, DIL_W), F32)],
        compiler_params=_params("parallel"),
        name="in_proj",
    )(x2, g_row, w_mix, cos_t, sin_t, qg_row, kg_row)


def _sb_body(q_ref, k_ref, v_ref, o_ref, z_a, z_b, w_a, w_b):
    tq = q_ref.shape[0]
    qi = pl.program_id(1)
    q = q_ref[...]
    head_of_lane = _iota((1, BRANCH_WIDTH), 1) // HEAD_DIM
    q_stack = jnp.concatenate([jnp.where(head_of_lane == h, q, jnp.zeros_like(q)) for h in range(N_HEADS)], axis=0)
    rows = _iota((tq, tq), 0)
    cols = _iota((tq, tq), 1)
    from_here = (rows >= cols).astype(BF16)
    causal = jnp.concatenate([cols < rows] * N_HEADS, axis=0)
    sign_bit = jnp.uint32(0x80000000)

    def scores(j):
        return _dot_nt(q_stack, k_ref[pl.ds(pl.multiple_of(j * tq, tq), tq), :])

    def weights(z, run, masked):
        neg_abs = lax.bitcast_convert_type(lax.bitcast_convert_type(z, jnp.uint32) | sign_bit, F32)
        sp = jnp.maximum(z, 0.0) + jnp.log(1.0 + jnp.exp2(neg_abs)) * LOG2_E
        if masked:
            sp = jnp.where(causal, sp, 0.0)
        tail = _dot(sp.astype(BF16), from_here)
        w = jnp.exp2((z - run) - tail)
        if masked:
            w = jnp.where(causal, w, 0.0)
        w = w.astype(BF16)
        w_cat = jnp.concatenate([w[h * tq:(h + 1) * tq] for h in range(N_HEADS)], axis=1)
        return w_cat, run + tail[:, 0:1]

    def apply(w_cat, j):
        vb = v_ref[pl.ds(pl.multiple_of(j * tq, tq), tq), :]
        v_stack = jnp.concatenate([jnp.where(head_of_lane == h, vb, jnp.zeros_like(vb)) for h in range(N_HEADS)], axis=0)
        return _dot(w_cat, v_stack)

    def step(j, z_cur, z_nxt, w_cur, w_prv, acc, run, masked=False):
        z_nxt[...] = scores(jnp.maximum(j - 1, 0))
        w_cat, run = weights(z_cur[...], run + jnp.where(j < 0, jnp.inf, 0.0), masked)
        w_cur[...] = w_cat
        if w_prv is not None:
            acc = acc + apply(w_prv[...], j + 1)
        return acc, run

    z_b[...] = scores(qi)
    acc, run = step(qi, z_b, z_a, w_b, None, jnp.zeros((tq, BRANCH_WIDTH), F32),
                    jnp.zeros((N_HEADS * tq, 1), F32), masked=True)

    def body(u, carry):
        acc, run = carry
        j = qi - 1 - 2 * u
        acc, run = step(j, z_a, z_b, w_a, w_b, acc, run)
        return step(j - 1, z_b, z_a, w_b, w_a, acc, run)

    acc, run = lax.fori_loop(0, (qi + 1) // 2, body, (acc, run))
    o_ref[...] = (acc + apply(w_b[...], 0)).astype(o_ref.dtype)


def _stick_breaking(sb3):
    b, s, _ = sb3.shape
    tq = SB_TILE
    whole = lambda blk: pl.BlockSpec((None, s, BRANCH_WIDTH), lambda bi, i: (bi, 0, blk), pipeline_mode=pl.Buffered(1))
    return pl.pallas_call(
        _sb_body,
        grid=(b, s // tq),
        in_specs=[pl.BlockSpec((None, tq, BRANCH_WIDTH), lambda bi, i: (bi, i, 0)), whole(1), whole(2)],
        out_specs=pl.BlockSpec((None, tq, BRANCH_WIDTH), lambda bi, i: (bi, i, 0)),
        out_shape=jax.ShapeDtypeStruct((b, s, BRANCH_WIDTH), BF16),
        scratch_shapes=[pltpu.VMEM((N_HEADS * tq, tq), F32), pltpu.VMEM((N_HEADS * tq, tq), F32),
                        pltpu.VMEM((tq, N_HEADS * tq), BF16), pltpu.VMEM((tq, N_HEADS * tq), BF16)],
        compiler_params=_params("parallel", "arbitrary"),
        name="stick_breaking",
    )(sb3, sb3, sb3)


def _gla_body(x_ref, w2_ref, b2_ref, g_ref, o_ref, state_ref):
    c = GLA_CHUNK
    nsub = c // GLA_SUB

    @pl.when(pl.program_id(1) == 0)
    def _():
        state_ref[...] = jnp.zeros_like(state_ref)

    ri = _iota((c, c), 0)
    ci = _iota((c, c), 1)
    tril_incl = (ci <= ri).astype(BF16)
    earlier_sub = (ci // GLA_SUB) < (ri // GLA_SUB)
    key_head = _iota((1, GLA_KEY_WIDTH), 1) // GLA_KEY_DIM
    val_head = _iota((1, BRANCH_WIDTH), 1) // HEAD_DIM
    key_to_val = _group_ones(GLA_KEY_WIDTH, BRANCH_WIDTH, GLA_KEY_DIM, HEAD_DIM)
    state_mask = key_to_val > 0
    head_ones = _group_ones(BRANCH_WIDTH, BRANCH_WIDTH, HEAD_DIM, HEAD_DIM)
    ones_cv = jnp.ones((c, BRANCH_WIDTH), BF16)
    sub_pos = _iota((1, GLA_SUB, 1), 1)
    scale = GLA_KEY_DIM ** -0.5

    def chunk(ic, carry):
        r0 = pl.multiple_of(ic * c, c)
        q = x_ref[pl.ds(r0, c), 0:128] * scale
        k = x_ref[pl.ds(r0, c), 128:256]
        v = x_ref[pl.ds(r0, c), 256:512]
        gate_r = x_ref[pl.ds(r0, c), 512:768]
        lr = x_ref[pl.ds(r0, c), 768:896]
        zg = jnp.dot(lr, w2_ref[...], precision=lax.Precision.HIGHEST, preferred_element_type=F32) + b2_ref[...]
        la = _log_sigmoid(zg) * (1.0 / GLA_GATE_NORM)
        bcum = _dot_exact_rhs(la, tril_incl, dot=lambda p, m: _dot(m, p), parts=3)
        btot_kv = _dot_exact_rhs(la, ones_cv, dot=_dot_tn, parts=3)
        b_last = bcum[c - 1:c, :]
        state = state_ref[...]

        qe = q * jnp.exp(bcum)
        ke = k * jnp.exp(-bcum)
        kl = k * jnp.exp(b_last - bcum)
        qe_b = qe.astype(BF16)
        ke_b = ke.astype(BF16)

        o = _dot(qe_b, state.astype(BF16))

        p_heads = []
        v_heads = []
        for h in range(N_HEADS):
            sc = _dot_nt(jnp.where(key_head == h, qe_b, jnp.zeros_like(qe_b)), ke_b)
            p_heads.append(jnp.where(earlier_sub, sc, 0.0).astype(BF16))
            v_heads.append(jnp.where(val_head == h, v, 0.0).astype(BF16))
        o = o + _dot(jnp.concatenate(p_heads, axis=1), jnp.concatenate(v_heads, axis=0))

        q3 = q.reshape(nsub, GLA_SUB, GLA_KEY_WIDTH)
        k3 = k.reshape(nsub, GLA_SUB, GLA_KEY_WIDTH)
        b3 = bcum.reshape(nsub, GLA_SUB, GLA_KEY_WIDTH)
        v3 = v.reshape(nsub, GLA_SUB, BRANCH_WIDTH)
        for jj in range(GLA_SUB):
            kj = k3[:, jj:jj + 1, :]
            bj = b3[:, jj:jj + 1, :]
            vj = jnp.broadcast_to(v3[:, jj:jj + 1, :], v3.shape).reshape(c, BRANCH_WIDTH)
            dec = jnp.exp(jnp.where(sub_pos >= jj, b3 - bj, -jnp.inf))
            pj = (q3 * kj * dec).reshape(c, GLA_KEY_WIDTH).astype(BF16)
            o = o + _dot(pj, key_to_val) * vj

        state_ref[...] = state * jnp.exp(btot_kv) + jnp.where(state_mask, _dot_tn(kl.astype(BF16), v.astype(BF16)), 0.0)
        y = _head_rms(o, g_ref[...], head_ones) * _silu(gate_r)
        o_ref[pl.ds(r0, c), :] = y.astype(o_ref.dtype)
        return carry

    lax.fori_loop(0, GLA_STEP // c, chunk, 0)


def _gla(gla3, w2_pad, b2_row, g_row):
    b, s, _ = gla3.shape
    return pl.pallas_call(
        _gla_body,
        grid=(b, s // GLA_STEP),
        in_specs=[pl.BlockSpec((None, GLA_STEP, GLA_W), lambda bi, i: (bi, i, 0)),
                  _resident((GLA_KEY_WIDTH, GLA_KEY_WIDTH)), _resident((1, GLA_KEY_WIDTH)),
                  _resident((1, BRANCH_WIDTH))],
        out_specs=pl.BlockSpec((None, GLA_STEP, BRANCH_WIDTH), lambda bi, i: (bi, i, 0)),
        out_shape=jax.ShapeDtypeStruct((b, s, BRANCH_WIDTH), BF16),
        scratch_shapes=[pltpu.VMEM((GLA_KEY_WIDTH, BRANCH_WIDTH), F32)],
        compiler_params=_params("parallel", "arbitrary"),
        name="gla",
    )(gla3, w2_pad, b2_row, g_row)


def _dn_body(x_ref, cw_ref, a_ref, dt_ref, g_ref, o_ref, xbuf_ref, state_ref):
    c = DN_CHUNK
    qkv_w = 3 * BRANCH_WIDTH

    @pl.when(pl.program_id(1) == 0)
    def _():
        state_ref[...] = jnp.zeros_like(state_ref)
        xbuf_ref[0:8, :] = jnp.zeros((8, qkv_w), F32)

    xbuf_ref[8:8 + c, :] = x_ref[:, 0:qkv_w]
    conv = cw_ref[DN_CONV - 1:DN_CONV, :] * xbuf_ref[8:8 + c, :]
    for i in range(DN_CONV - 1):
        conv = conv + cw_ref[i:i + 1, :] * xbuf_ref[pl.ds(8 - (DN_CONV - 1) + i, c), :]
    xbuf_ref[0:8, :] = xbuf_ref[c:c + 8, :]
    act = _silu(conv)

    head_ones = _group_ones(BRANCH_WIDTH, BRANCH_WIDTH, HEAD_DIM, HEAD_DIM)

    def l2n(t):
        return t * lax.rsqrt(_dot_exact_rhs(t * t, head_ones) + EPS)

    q_all = l2n(act[:, 0:BRANCH_WIDTH]) * (HEAD_DIM ** -0.5)
    k_all = l2n(act[:, BRANCH_WIDTH:2 * BRANCH_WIDTH])
    v_all = act[:, 2 * BRANCH_WIDTH:]

    ab = x_ref[:, qkv_w + BRANCH_WIDTH:]
    g_tok = a_ref[...] * _softplus(ab + dt_ref[...])
    beta_all = jax.nn.sigmoid(ab)

    ri = _iota((c, c), 0)
    ci = _iota((c, c), 1)
    incl = ci <= ri
    strict = ci < ri
    g_col = _dot_exact_rhs(g_tok, incl.astype(BF16), dot=lambda p, m: _dot(m, p), parts=3)
    g_row = _dot_exact_rhs(g_tok, (ri <= ci).astype(BF16), dot=_dot_tn, parts=3)
    eye = (ri == ci).astype(F32)
    level_masks = []
    s = 1
    while s < c:
        level_masks.append((ri // (2 * s) == ci // (2 * s)) & (ri % (2 * s) >= s) & (ci % (2 * s) < s))
        s *= 2

    outs = []
    for h in range(N_HEADS):
        sl = slice(h * HEAD_DIM, (h + 1) * HEAD_DIM)
        q, k, v = q_all[:, sl], k_all[:, sl], v_all[:, sl]
        beta = beta_all[:, N_HEADS + h:N_HEADS + h + 1]
        gc = g_col[:, h:h + 1]
        gr = g_row[h:h + 1, :]
        g_last = g_col[c - 1:c, h:h + 1]
        decay = jnp.exp(jnp.where(incl, gc - gr, -jnp.inf))
        kb = k * beta
        lower = jnp.where(strict, _dot3(kb, k, dot=_dot_nt) * decay, 0.0)
        tinv = eye - jnp.where(level_masks[0], lower, 0.0)
        for m in level_masks[1:]:
            tinv = tinv - _dot3(_dot3(tinv, jnp.where(m, lower, 0.0)), tinv)
        egc = jnp.exp(gc)
        sol = _dot3(tinv, jnp.concatenate([v * beta, kb * egc], axis=1))
        u, w = sol[:, 0:HEAD_DIM], sol[:, HEAD_DIM:]
        scores = jnp.where(incl, _dot_nt(q.astype(BF16), k.astype(BF16)) * decay, 0.0)
        q_in = q * egc
        k_up = k * jnp.exp(g_last - gc)
        state = state_ref[h]
        state_b = state.astype(BF16)
        v_new = u - _dot(w.astype(BF16), state_b)
        o_h = _dot(q_in.astype(BF16), state_b) + _dot(scores.astype(BF16), v_new.astype(BF16))
        state_ref[h] = jnp.exp(g_last) * state + _dot_tn(k_up.astype(BF16), v_new.astype(BF16))
        outs.append(o_h)

    o = jnp.concatenate(outs, axis=1)
    gate = x_ref[:, qkv_w:qkv_w + BRANCH_WIDTH]
    o_ref[...] = (_head_rms(o, g_ref[...], head_ones) * _silu(gate)).astype(o_ref.dtype)


def _deltanet(dn3, conv_w_t, a_row, dt_row, g_row):
    b, s, _ = dn3.shape
    c = DN_CHUNK
    return pl.pallas_call(
        _dn_body,
        grid=(b, s // c),
        in_specs=[pl.BlockSpec((None, c, DN_W), lambda bi, i: (bi, i, 0)),
                  _resident((DN_CONV, 3 * BRANCH_WIDTH)), _resident((1, 128)), _resident((1, 128)),
                  _resident((1, BRANCH_WIDTH))],
        out_specs=pl.BlockSpec((None, c, BRANCH_WIDTH), lambda bi, i: (bi, i, 0)),
        out_shape=jax.ShapeDtypeStruct((b, s, BRANCH_WIDTH), BF16),
        scratch_shapes=[pltpu.VMEM((c + 8, 3 * BRANCH_WIDTH), F32), pltpu.VMEM((N_HEADS, HEAD_DIM, HEAD_DIM), F32)],
        compiler_params=_params("parallel", "arbitrary"),
        name="deltanet",
    )(dn3, conv_w_t, a_row, dt_row, g_row)


def _ld_halves(ref, rows):
    return jnp.concatenate([ref[0, rows, :], ref[1, rows, :]], axis=1)


def _st_halves(ref, rows, val):
    ref[0, rows, :] = val[:, 0:128]
    ref[1, rows, :] = val[:, 128:256]


def _dil_body(q_ref, kc_ref, kp_ref, vc_ref, vp_ref, o_ref, qx_ref, kx_ref, vx_ref, m_ref, l_ref, acc_ref):
    n = pl.program_id(1)
    sup = DIL_SUPER
    blk = DIL_BLOCK
    _st_halves(qx_ref, slice(0, sup), q_ref[...])
    _st_halves(kx_ref, slice(0, sup), kp_ref[...])
    _st_halves(kx_ref, slice(sup, 2 * sup), kc_ref[...])
    _st_halves(vx_ref, slice(0, sup), vp_ref[...])
    _st_halves(vx_ref, slice(sup, 2 * sup), vc_ref[...])

    head_of_lane = _iota((1, BRANCH_WIDTH), 1) // HEAD_DIM
    qi = _iota((blk, 2 * blk), 0)
    kj = _iota((blk, 2 * blk), 1)
    dist = qi + blk - kj
    band = (dist >= 0) & (dist <= DIL_KEYS)
    in_current = kj >= blk

    for p, dil in enumerate(DIL_DILATIONS):
        def qblock(qb, carry, p=p, dil=dil):
            r = qb % dil
            m = qb // dil
            q_start = r + dil * blk * m
            k_start = sup + r + dil * blk * (m - 1)
            if dil == 1:
                q_rows = pl.ds(q_start, blk)
                k_rows = pl.ds(k_start, 2 * blk)
            else:
                q_rows = pl.ds(q_start, blk, stride=dil)
                k_rows = pl.ds(k_start, 2 * blk, stride=dil)
            q = _ld_halves(qx_ref, q_rows).astype(BF16)
            k = _ld_halves(kx_ref, k_rows).astype(BF16)
            v = _ld_halves(vx_ref, k_rows).astype(BF16)
            has_prev = jnp.logical_or(n > 0, m > 0)
            valid = band & jnp.logical_or(in_current, has_prev)
            mx_t = den_t = num = None
            for h in range(N_HEADS):
                s = _dot_nt(jnp.where(head_of_lane == h, q, jnp.zeros_like(q)), k)
                s = jnp.where(valid, s, -jnp.inf)
                mx = jnp.max(s, axis=-1, keepdims=True)
                pr = jnp.exp(s - mx)
                den = jnp.sum(pr, axis=-1, keepdims=True)
                num_h = _dot(pr.astype(BF16), v)
                sel = head_of_lane == h
                if h == 0:
                    mx_t = jnp.broadcast_to(mx, (blk, BRANCH_WIDTH))
                    den_t = jnp.broadcast_to(den, (blk, BRANCH_WIDTH))
                    num = num_h
                else:
                    mx_t = jnp.where(sel, mx, mx_t)
                    den_t = jnp.where(sel, den, den_t)
                    num = jnp.where(sel, num_h, num)
            if p == 0:
                _st_halves(m_ref, q_rows, mx_t)
                _st_halves(l_ref, q_rows, den_t)
                _st_halves(acc_ref, q_rows, num)
            else:
                m_old = _ld_halves(m_ref, q_rows)
                m_new = jnp.maximum(m_old, mx_t)
                a_old = jnp.exp(m_old - m_new)
                a_new = jnp.exp(mx_t - m_new)
                _st_halves(m_ref, q_rows, m_new)
                _st_halves(l_ref, q_rows, a_old * _ld_halves(l_ref, q_rows) + a_new * den_t)
                _st_halves(acc_ref, q_rows, a_old * _ld_halves(acc_ref, q_rows) + a_new * num)
            return carry

        lax.fori_loop(0, sup // blk, qblock, 0)

    whole = slice(0, sup)
    o_ref[...] = (_ld_halves(acc_ref, whole) / _ld_halves(l_ref, whole)).astype(o_ref.dtype)


def _dilated(dil3):
    b, s, _ = dil3.shape
    sup = DIL_SUPER
    cur = lambda blk: pl.BlockSpec((None, sup, BRANCH_WIDTH), lambda bi, i: (bi, i, blk))
    prev = lambda blk: pl.BlockSpec((None, sup, BRANCH_WIDTH), lambda bi, i: (bi, jnp.maximum(i - 1, 0), blk))
    return pl.pallas_call(
        _dil_body,
        grid=(b, s // sup),
        in_specs=[cur(0), cur(1), prev(1), cur(2), prev(2)],
        out_specs=pl.BlockSpec((None, sup, BRANCH_WIDTH), lambda bi, i: (bi, i, 0)),
        out_shape=jax.ShapeDtypeStruct((b, s, BRANCH_WIDTH), BF16),
        scratch_shapes=[pltpu.VMEM((2, sup, 128), F32), pltpu.VMEM((2, 2 * sup, 128), F32),
                        pltpu.VMEM((2, 2 * sup, 128), F32), pltpu.VMEM((2, sup, 128), F32),
                        pltpu.VMEM((2, sup, 128), F32), pltpu.VMEM((2, sup, 128), F32)],
        compiler_params=_params("parallel", "arbitrary"),
        name="dilated",
    )(dil3, dil3, dil3, dil3, dil3)


def _merge_body(x_ref, g_ref, wg_ref, o0_ref, o1_ref, o2_ref, o3_ref, wb_ref, wo_ref, out_ref):
    x = x_ref[...]
    d_model = x.shape[-1]
    h = (x * lax.rsqrt(jnp.mean(x * x, axis=-1, keepdims=True) + EPS) * g_ref[...]).astype(BF16)
    merged = None
    for nb, o_ref in enumerate((o0_ref, o1_ref, o2_ref, o3_ref)):
        gate = jax.nn.sigmoid(_dot(h, wg_ref[:, nb * d_model:(nb + 1) * d_model]))
        term = gate * _dot(o_ref[...], wb_ref[nb])
        merged = term if merged is None else merged + term
    out_ref[...] = x + _dot(merged.astype(BF16), wo_ref[...])


def _merge(x2, g_row, w_gate, branches, w_branch, w_out):
    t, d_model = x2.shape
    tm = ROW_TILE
    row = lambda w: pl.BlockSpec((tm, w), lambda i: (i, 0))
    return pl.pallas_call(
        _merge_body,
        grid=(t // tm,),
        in_specs=[row(d_model), _resident((1, d_model)), _resident((d_model, N_BRANCH * d_model))]
        + [row(BRANCH_WIDTH)] * N_BRANCH
        + [_resident((N_BRANCH, BRANCH_WIDTH, d_model)), _resident((d_model, d_model))],
        out_specs=row(d_model),
        out_shape=jax.ShapeDtypeStruct((t, d_model), F32),
        compiler_params=_params("parallel"),
        name="merge_out",
    )(x2, g_row, w_gate, *branches, w_branch, w_out)


def _mlp_body(x_ref, g_ref, w1_ref, w2_ref, out_ref):
    x = x_ref[...]
    h = (x * lax.rsqrt(jnp.mean(x * x, axis=-1, keepdims=True) + EPS) * g_ref[...]).astype(BF16)
    acc = x
    for c0 in range(0, w1_ref.shape[1], FF_TILE):
        a = jnp.maximum(_dot(h, w1_ref[:, c0:c0 + FF_TILE]), 0.0)
        acc = acc + _dot((a * a).astype(BF16), w2_ref[c0:c0 + FF_TILE, :])
    out_ref[...] = acc


def _mlp(x2, g_row, w1, w2):
    t, d_model = x2.shape
    d_ff = w1.shape[1]
    tm = ROW_TILE
    row = pl.BlockSpec((tm, d_model), lambda i: (i, 0))
    return pl.pallas_call(
        _mlp_body,
        grid=(t // tm,),
        in_specs=[row, _resident((1, d_model)), _resident((d_model, d_ff)), _resident((d_ff, d_model))],
        out_specs=row,
        out_shape=jax.ShapeDtypeStruct((t, d_model), F32),
        compiler_params=_params("parallel"),
        name="mlp",
    )(x2, g_row, w1, w2)


def _mixer_weights(w_in):
    d_model = w_in.shape[0]
    bw, kw = BRANCH_WIDTH, GLA_KEY_WIDTH
    o = 0
    sb = w_in[:, o:o + 3 * bw]
    o += 3 * bw
    gla_q, gla_k = w_in[:, o:o + kw], w_in[:, o + kw:o + 2 * kw]
    o += 2 * kw
    gla_v = w_in[:, o:o + bw]
    o += bw
    gla_lr = w_in[:, o:o + GLA_LOW_RANK]
    o += GLA_LOW_RANK
    gla_r = w_in[:, o:o + bw]
    o += bw
    dn_qkv = w_in[:, o:o + 3 * bw]
    o += 3 * bw
    dn_ab = w_in[:, o:o + 2 * N_HEADS]
    o += 2 * N_HEADS
    dn_gate = w_in[:, o:o + bw]
    o += bw
    dil = w_in[:, o:o + 3 * bw]
    o += 3 * bw
    zeros = lambda n: jnp.zeros((d_model, n), w_in.dtype)
    gla = jnp.concatenate([gla_q, gla_k, gla_v, gla_r, gla_lr, zeros(GLA_W - 3 * bw - GLA_LOW_RANK)], axis=1)
    dn = jnp.concatenate([dn_qkv, dn_gate, dn_ab, zeros(DN_W - 4 * bw - 2 * N_HEADS)], axis=1)
    return jnp.concatenate([sb, gla, dn, dil], axis=1).astype(BF16), o


def _rope_tables(seq):
    half = ROT_DIM // 2
    inv_freq = ROPE_THETA ** (-jnp.arange(half, dtype=F32) / half)
    ang = jnp.arange(seq, dtype=jnp.int32).astype(F32)[:, None] * inv_freq[None, :]
    cos, sin = jnp.cos(ang), jnp.sin(ang)
    rest = HEAD_DIM - ROT_DIM
    cos_h = jnp.concatenate([cos, cos, jnp.ones((seq, rest), F32)], axis=1)
    sin_h = jnp.concatenate([-sin, sin, jnp.zeros((seq, rest), F32)], axis=1)
    return jnp.tile(cos_h, (1, N_HEADS)), jnp.tile(sin_h, (1, N_HEADS))


def kernel(x, norm_mix_g, norm_mlp_g, w_in, gla_w_lr2, gla_b_lr, gla_norm_g, dn_conv_w, dn_a_log, dn_dt_bias,
           dn_norm_g, dil_q_norm_g, dil_k_norm_g, w_branch, w_out, w_mlp_in, w_mlp_out):
    b, s, d_model = x.shape
    depth = w_in.shape[0]
    t = b * s
    assert t % ROW_TILE == 0 and s % ROW_TILE == 0 and s % DIL_SUPER == 0 and s % SB_TILE == 0
    cos_t, sin_t = _rope_tables(s)
    tile_heads = lambda g: jnp.tile(g, N_HEADS)[None, :]
    pad_lanes = lambda v: jnp.pad(v, (0, 128 - v.shape[0]))[None, :]

    x2 = x.reshape(t, d_model)
    for l in range(depth):
        w_mix, n_mix = _mixer_weights(w_in[l])
        w_gate = w_in[l][:, n_mix:].astype(BF16)
        sb, gla, dn, dil = _in_proj(x2, norm_mix_g[l][None, :], w_mix, cos_t, sin_t,
                                    tile_heads(dil_q_norm_g[l]), tile_heads(dil_k_norm_g[l]), s)
        o_sb = _stick_breaking(sb.reshape(b, s, SB_W))
        w2_pad = jnp.pad(gla_w_lr2[l], ((0, GLA_KEY_WIDTH - GLA_LOW_RANK), (0, 0)))
        o_gla = _gla(gla.reshape(b, s, GLA_W), w2_pad, gla_b_lr[l][None, :], tile_heads(gla_norm_g[l]))
        o_dn = _deltanet(dn.reshape(b, s, DN_W), dn_conv_w[l].T, pad_lanes(-jnp.exp(dn_a_log[l])),
                         pad_lanes(dn_dt_bias[l]), tile_heads(dn_norm_g[l]))
        o_dil = _dilated(dil.reshape(b, s, DIL_W))
        branches = [o.reshape(t, BRANCH_WIDTH) for o in (o_sb, o_gla, o_dn, o_dil)]
        x2 = _merge(x2, norm_mix_g[l][None, :], w_gate, branches, w_branch[l].astype(BF16), w_out[l].astype(BF16))
        x2 = _mlp(x2, norm_mlp_g[l][None, :], w_mlp_in[l].astype(BF16), w_mlp_out[l].astype(BF16))
    return x2.reshape(b, s, d_model)
```

```python
import itertools

import jax
import jax.numpy as jnp
from jax import lax
from jax.experimental import pallas as pl
from jax.experimental.pallas import tpu as pltpu

F32 = jnp.float32
BF16 = jnp.bfloat16

HEAD_DIM = 64
N_HEADS = 4
BRANCH_WIDTH = N_HEADS * HEAD_DIM
GLA_KEY_DIM = 32
GLA_LOW_RANK = 16
GLA_GATE_NORM = 16.0
GLA_KEY_WIDTH = N_HEADS * GLA_KEY_DIM
DN_CONV = 4
DIL_DILATIONS = (1, 4, 16)
DIL_KEYS = 128
ROPE_THETA = 500000.0
ROT_DIM = HEAD_DIM // 4
N_BRANCH = 4
EPS = 1e-6
LOG2_E = 1.4426950408889634

SB_W = 3 * BRANCH_WIDTH
GLA_W = 896
DN_W = 1152
DIL_W = 3 * BRANCH_WIDTH
MIX_W = SB_W + GLA_W + DN_W + DIL_W

ROW_TILE = 512
SB_TILE = 256
GLA_CHUNK = 64
GLA_SUB = 16
GLA_STEP = 256
DN_CHUNK = 64
DIL_BLOCK = 128
DIL_SUPER = DIL_BLOCK * DIL_DILATIONS[-1]
DIL_GROUP = 2
FF_TILE = 1024

V7X_VMEM_LIMIT = 56 * 1024 * 1024


def _dot(a, b):
    return jnp.dot(a, b, preferred_element_type=F32)


def _dot_nt(a, b):
    return lax.dot_general(a, b, (((1,), (1,)), ((), ())), preferred_element_type=F32)


def _dot_tn(a, b):
    return lax.dot_general(a, b, (((0,), (0,)), ((), ())), preferred_element_type=F32)


def _split2(a):
    hi = a.astype(BF16)
    lo = (a - hi.astype(F32)).astype(BF16)
    return hi, lo


def _split3(a):
    hi = a.astype(BF16)
    r = a - hi.astype(F32)
    mid = r.astype(BF16)
    lo = (r - mid.astype(F32)).astype(BF16)
    return hi, mid, lo


def _dot3(a, b, dot=_dot):
    ah, al = _split2(a)
    bh, bl = _split2(b)
    return dot(ah, bh) + (dot(ah, bl) + dot(al, bh))


def _dot_exact_rhs(a, b_exact, dot=_dot, parts=2):
    pieces = _split2(a) if parts == 2 else _split3(a)
    out = dot(pieces[0], b_exact)
    for p in pieces[1:]:
        out = out + dot(p, b_exact)
    return out


def _iota(shape, dim):
    return lax.broadcasted_iota(jnp.int32, shape, dim)


def _group_ones(rows, cols, rgroup, cgroup):
    return (_iota((rows, cols), 0) // rgroup == _iota((rows, cols), 1) // cgroup).astype(BF16)


def _log_sigmoid(z):
    return jnp.minimum(z, 0.0) - jnp.log1p(jnp.exp(-jnp.abs(z)))


def _softplus(z):
    return jnp.maximum(z, 0.0) + jnp.log1p(jnp.exp(-jnp.abs(z)))


def _silu(z):
    return z * jax.nn.sigmoid(z)


def _head_rms(x, gain_row, head_ones):
    ms = _dot_exact_rhs(x * x, head_ones) * (1.0 / HEAD_DIM)
    return x * lax.rsqrt(ms + EPS) * gain_row


def _params(*semantics):
    return pltpu.CompilerParams(dimension_semantics=semantics, vmem_limit_bytes=V7X_VMEM_LIMIT)


def _resident(shape):
    nd = len(shape)
    return pl.BlockSpec(shape, lambda *_: (0,) * nd, pipeline_mode=pl.Buffered(1))


def _in_proj_body(x_ref, g_ref, w_ref, cos_ref, sin_ref, qg_ref, kg_ref, sb_ref, gla_ref, dn_ref, dil_ref):
    x = x_ref[...]
    h = (x * lax.rsqrt(jnp.mean(x * x, axis=-1, keepdims=True) + EPS) * g_ref[...]).astype(BF16)
    q_scale = jnp.where(_iota((1, SB_W), 1) < BRANCH_WIDTH, LOG2_E * HEAD_DIM ** -0.5, 1.0)
    sb_ref[...] = (_dot(h, w_ref[:, 0:SB_W]) * q_scale).astype(BF16)
    off = SB_W
    gla_ref[...] = _dot(h, w_ref[:, off:off + GLA_W])
    off += GLA_W
    dn_ref[...] = _dot(h, w_ref[:, off:off + DN_W])
    off += DN_W
    d = _dot(h, w_ref[:, off:off + DIL_W])

    head_ones = _group_ones(BRANCH_WIDTH, BRANCH_WIDTH, HEAD_DIM, HEAD_DIM)
    lane = _iota((1, BRANCH_WIDTH), 1) % HEAD_DIM
    first_half = lane < ROT_DIM // 2
    cos = cos_ref[...]
    sin = sin_ref[...]

    def norm_rope(t, gain_row):
        y = _head_rms(t, gain_row, head_ones)
        partner = jnp.where(first_half, pltpu.roll(y, BRANCH_WIDTH - ROT_DIM // 2, 1), pltpu.roll(y, ROT_DIM // 2, 1))
        return y * cos + partner * sin

    dil_ref[:, 0:BRANCH_WIDTH] = norm_rope(d[:, 0:BRANCH_WIDTH], qg_ref[...]) * (HEAD_DIM ** -0.5)
    dil_ref[:, BRANCH_WIDTH:2 * BRANCH_WIDTH] = norm_rope(d[:, BRANCH_WIDTH:2 * BRANCH_WIDTH], kg_ref[...])
    dil_ref[:, 2 * BRANCH_WIDTH:] = d[:, 2 * BRANCH_WIDTH:]


def _in_proj(x2, g_row, w_mix, cos_t, sin_t, qg_row, kg_row, seq):
    t, d_model = x2.shape
    tm = ROW_TILE
    n_seq_tiles = seq // tm
    row = lambda w: pl.BlockSpec((tm, w), lambda i: (i, 0))
    tab = pl.BlockSpec((tm, BRANCH_WIDTH), lambda i: (i % n_seq_tiles, 0))
    return pl.pallas_call(
        _in_proj_body,
        grid=(t // tm,),
        in_specs=[row(d_model), _resident((1, d_model)), _resident((d_model, MIX_W)), tab, tab,
                  _resident((1, BRANCH_WIDTH)), _resident((1, BRANCH_WIDTH))],
        out_specs=[row(SB_W), row(GLA_W), row(DN_W), row(DIL_W)],
        out_shape=[jax.ShapeDtypeStruct((t, SB_W), BF16), jax.ShapeDtypeStruct((t, GLA_W), F32),
                   jax.ShapeDtypeStruct((t, DN_W), F32), jax.ShapeDtypeStruct((t, DIL_W), F32)],
        compiler_params=_params("parallel"),
        name="in_proj",
    )(x2, g_row, w_mix, cos_t, sin_t, qg_row, kg_row)


def _sb_body(q_ref, k_ref, v_ref, o_ref, z_a, z_b, w_a, w_b):
    tq = q_ref.shape[0]
    qi = pl.program_id(1)
    q = q_ref[...]
    head_of_lane = _iota((1, BRANCH_WIDTH), 1) // HEAD_DIM
    q_stack = jnp.concatenate([jnp.where(head_of_lane == h, q, jnp.zeros_like(q)) for h in range(N_HEADS)], axis=0)
    rows = _iota((tq, tq), 0)
    cols = _iota((tq, tq), 1)
    from_here = (rows >= cols).astype(BF16)
    causal = jnp.concatenate([cols < rows] * N_HEADS, axis=0)
    sign_bit = jnp.uint32(0x80000000)

    def scores(j):
        return _dot_nt(q_stack, k_ref[pl.ds(pl.multiple_of(j * tq, tq), tq), :])

    def weights(z, run, masked):
        neg_abs = lax.bitcast_convert_type(lax.bitcast_convert_type(z, jnp.uint32) | sign_bit, F32)
        sp = jnp.maximum(z, 0.0) + jnp.log(1.0 + jnp.exp2(neg_abs)) * LOG2_E
        if masked:
            sp = jnp.where(causal, sp, 0.0)
        tail = _dot(sp.astype(BF16), from_here)
        w = jnp.exp2((z - run) - tail)
        if masked:
            w = jnp.where(causal, w, 0.0)
        w = w.astype(BF16)
        w_cat = jnp.concatenate([w[h * tq:(h + 1) * tq] for h in range(N_HEADS)], axis=1)
        return w_cat, run + tail[:, 0:1]

    def apply(w_cat, j):
        vb = v_ref[pl.ds(pl.multiple_of(j * tq, tq), tq), :]
        v_stack = jnp.concatenate([jnp.where(head_of_lane == h, vb, jnp.zeros_like(vb)) for h in range(N_HEADS)], axis=0)
        return _dot(w_cat, v_stack)

    def step(j, z_cur, z_nxt, w_cur, w_prv, acc, run, masked=False):
        z_nxt[...] = scores(jnp.maximum(j - 1, 0))
        w_cat, run = weights(z_cur[...], run + jnp.where(j < 0, jnp.inf, 0.0), masked)
        w_cur[...] = w_cat
        if w_prv is not None:
            acc = acc + apply(w_prv[...], j + 1)
        return acc, run

    z_b[...] = scores(qi)
    acc, run = step(qi, z_b, z_a, w_b, None, jnp.zeros((tq, BRANCH_WIDTH), F32),
                    jnp.zeros((N_HEADS * tq, 1), F32), masked=True)

    def body(u, carry):
        acc, run = carry
        j = qi - 1 - 2 * u
        acc, run = step(j, z_a, z_b, w_a, w_b, acc, run)
        return step(j - 1, z_b, z_a, w_b, w_a, acc, run)

    acc, run = lax.fori_loop(0, (qi + 1) // 2, body, (acc, run))
    o_ref[...] = (acc + apply(w_b[...], 0)).astype(o_ref.dtype)


def _stick_breaking(sb3):
    b, s, _ = sb3.shape
    tq = SB_TILE
    whole = lambda blk: pl.BlockSpec((None, s, BRANCH_WIDTH), lambda bi, i: (bi, 0, blk), pipeline_mode=pl.Buffered(1))
    return pl.pallas_call(
        _sb_body,
        grid=(b, s // tq),
        in_specs=[pl.BlockSpec((None, tq, BRANCH_WIDTH), lambda bi, i: (bi, i, 0)), whole(1), whole(2)],
        out_specs=pl.BlockSpec((None, tq, BRANCH_WIDTH), lambda bi, i: (bi, i, 0)),
        out_shape=jax.ShapeDtypeStruct((b, s, BRANCH_WIDTH), BF16),
        scratch_shapes=[pltpu.VMEM((N_HEADS * tq, tq), F32), pltpu.VMEM((N_HEADS * tq, tq), F32),
                        pltpu.VMEM((tq, N_HEADS * tq), BF16), pltpu.VMEM((tq, N_HEADS * tq), BF16)],
        compiler_params=_params("parallel", "arbitrary"),
        name="stick_breaking",
    )(sb3, sb3, sb3)


def _gla_body(x_ref, w2_ref, b2_ref, g_ref, o_ref, state_ref):
    c = GLA_CHUNK
    nsub = c // GLA_SUB

    @pl.when(pl.program_id(1) == 0)
    def _():
        state_ref[...] = jnp.zeros_like(state_ref)

    ri = _iota((c, c), 0)
    ci = _iota((c, c), 1)
    tril_incl = (ci <= ri).astype(BF16)
    earlier_sub = (ci // GLA_SUB) < (ri // GLA_SUB)
    key_head = _iota((1, GLA_KEY_WIDTH), 1) // GLA_KEY_DIM
    val_head = _iota((1, BRANCH_WIDTH), 1) // HEAD_DIM
    key_to_val = _group_ones(GLA_KEY_WIDTH, BRANCH_WIDTH, GLA_KEY_DIM, HEAD_DIM)
    state_mask = key_to_val > 0
    head_ones = _group_ones(BRANCH_WIDTH, BRANCH_WIDTH, HEAD_DIM, HEAD_DIM)
    ones_cv = jnp.ones((c, BRANCH_WIDTH), BF16)
    sub_pos = _iota((1, GLA_SUB, 1), 1)
    scale = GLA_KEY_DIM ** -0.5

    def chunk(ic, carry):
        r0 = pl.multiple_of(ic * c, c)
        q = x_ref[pl.ds(r0, c), 0:128] * scale
        k = x_ref[pl.ds(r0, c), 128:256]
        v = x_ref[pl.ds(r0, c), 256:512]
        gate_r = x_ref[pl.ds(r0, c), 512:768]
        lr = x_ref[pl.ds(r0, c), 768:896]
        zg = jnp.dot(lr, w2_ref[...], precision=lax.Precision.HIGHEST, preferred_element_type=F32) + b2_ref[...]
        la = _log_sigmoid(zg) * (1.0 / GLA_GATE_NORM)
        bcum = _dot_exact_rhs(la, tril_incl, dot=lambda p, m: _dot(m, p), parts=3)
        btot_kv = _dot_exact_rhs(la, ones_cv, dot=_dot_tn, parts=3)
        b_last = bcum[c - 1:c, :]
        state = state_ref[...]

        qe = q * jnp.exp(bcum)
        ke = k * jnp.exp(-bcum)
        kl = k * jnp.exp(b_last - bcum)
        qe_b = qe.astype(BF16)
        ke_b = ke.astype(BF16)

        o = _dot(qe_b, state.astype(BF16))

        p_heads = []
        v_heads = []
        for h in range(N_HEADS):
            sc = _dot_nt(jnp.where(key_head == h, qe_b, jnp.zeros_like(qe_b)), ke_b)
            p_heads.append(jnp.where(earlier_sub, sc, 0.0).astype(BF16))
            v_heads.append(jnp.where(val_head == h, v, 0.0).astype(BF16))
        o = o + _dot(jnp.concatenate(p_heads, axis=1), jnp.concatenate(v_heads, axis=0))

        q3 = q.reshape(nsub, GLA_SUB, GLA_KEY_WIDTH)
        k3 = k.reshape(nsub, GLA_SUB, GLA_KEY_WIDTH)
        b3 = bcum.reshape(nsub, GLA_SUB, GLA_KEY_WIDTH)
        v3 = v.reshape(nsub, GLA_SUB, BRANCH_WIDTH)
        for jj in range(GLA_SUB):
            kj = k3[:, jj:jj + 1, :]
            bj = b3[:, jj:jj + 1, :]
            vj = jnp.broadcast_to(v3[:, jj:jj + 1, :], v3.shape).reshape(c, BRANCH_WIDTH)
            dec = jnp.exp(jnp.where(sub_pos >= jj, b3 - bj, -jnp.inf))
            pj = (q3 * kj * dec).reshape(c, GLA_KEY_WIDTH).astype(BF16)
            o = o + _dot(pj, key_to_val) * vj

        state_ref[...] = state * jnp.exp(btot_kv) + jnp.where(state_mask, _dot_tn(kl.astype(BF16), v.astype(BF16)), 0.0)
        y = _head_rms(o, g_ref[...], head_ones) * _silu(gate_r)
        o_ref[pl.ds(r0, c), :] = y.astype(o_ref.dtype)
        return carry

    lax.fori_loop(0, GLA_STEP // c, chunk, 0)


def _gla(gla3, w2_pad, b2_row, g_row):
    b, s, _ = gla3.shape
    return pl.pallas_call(
        _gla_body,
        grid=(b, s // GLA_STEP),
        in_specs=[pl.BlockSpec((None, GLA_STEP, GLA_W), lambda bi, i: (bi, i, 0)),
                  _resident((GLA_KEY_WIDTH, GLA_KEY_WIDTH)), _resident((1, GLA_KEY_WIDTH)),
                  _resident((1, BRANCH_WIDTH))],
        out_specs=pl.BlockSpec((None, GLA_STEP, BRANCH_WIDTH), lambda bi, i: (bi, i, 0)),
        out_shape=jax.ShapeDtypeStruct((b, s, BRANCH_WIDTH), BF16),
        scratch_shapes=[pltpu.VMEM((GLA_KEY_WIDTH, BRANCH_WIDTH), F32)],
        compiler_params=_params("parallel", "arbitrary"),
        name="gla",
    )(gla3, w2_pad, b2_row, g_row)


def _dn_body(x_ref, cw_ref, a_ref, dt_ref, g_ref, incl_ref, strict_ref, level_ref, o_ref, xbuf_ref, state_ref):
    c = DN_CHUNK
    n = N_HEADS * c
    qkv_w = 3 * BRANCH_WIDTH

    @pl.when(pl.program_id(0) == 0)
    def _():
        state_ref[...] = jnp.zeros_like(state_ref)
        xbuf_ref[:, 0:8, :] = jnp.zeros((x_ref.shape[0], 8, qkv_w), F32)

    head_ones = _group_ones(BRANCH_WIDTH, BRANCH_WIDTH, HEAD_DIM, HEAD_DIM)
    head_of_lane = _iota((1, BRANCH_WIDTH), 1) // HEAD_DIM
    ri = _iota((c, c), 0)
    ci = _iota((c, c), 1)
    cum_incl = (ci <= ri).astype(BF16)
    cum_incl_t = (ri <= ci).astype(BF16)
    eye = (_iota((n, n), 0) == _iota((n, n), 1)).astype(F32)

    def l2n(t):
        return t * lax.rsqrt(_dot_exact_rhs(t * t, head_ones) + EPS)

    def stack(t):
        return jnp.concatenate([jnp.where(head_of_lane == h, t, 0.0) for h in range(N_HEADS)], axis=0)

    def stack_col(t, lane0):
        return jnp.concatenate([t[:, lane0 + h:lane0 + h + 1] for h in range(N_HEADS)], axis=0)

    def chunk_of_row(b):
        xbuf_ref[b, 8:8 + c, :] = x_ref[b, :, 0:qkv_w]
        conv = cw_ref[DN_CONV - 1:DN_CONV, :] * xbuf_ref[b, 8:8 + c, :]
        for i in range(DN_CONV - 1):
            conv = conv + cw_ref[i:i + 1, :] * xbuf_ref[b, pl.ds(8 - (DN_CONV - 1) + i, c), :]
        xbuf_ref[b, 0:8, :] = xbuf_ref[b, c:c + 8, :]
        act = _silu(conv)
        q_all = l2n(act[:, 0:BRANCH_WIDTH]) * (HEAD_DIM ** -0.5)
        k_all = l2n(act[:, BRANCH_WIDTH:2 * BRANCH_WIDTH])
        v_all = act[:, 2 * BRANCH_WIDTH:]

        ab = x_ref[b, :, qkv_w + BRANCH_WIDTH:]
        g_tok = a_ref[...] * _softplus(ab + dt_ref[...])
        beta_all = jax.nn.sigmoid(ab)
        g_col = _dot_exact_rhs(g_tok, cum_incl, dot=lambda p, m: _dot(m, p), parts=3)
        g_row = _dot_exact_rhs(g_tok, cum_incl_t, dot=_dot_tn, parts=3)
        yield

        gc = stack_col(g_col, 0)
        gr = jnp.concatenate([g_row[h:h + 1, :] for h in range(N_HEADS)], axis=1)
        g_last = jnp.concatenate([jnp.broadcast_to(g_col[c - 1:c, h:h + 1], (c, 1)) for h in range(N_HEADS)], axis=0)
        beta = stack_col(beta_all, N_HEADS)
        egc = jnp.exp(gc)
        q_s, k_s, v_s = stack(q_all), stack(k_all), stack(v_all)
        kb_s = k_s * beta
        decay = jnp.exp(jnp.minimum(gc - gr, 0.0))
        lower = _dot3(kb_s, k_s, dot=_dot_nt) * decay * strict_ref[...]
        scores = _dot_nt(q_s.astype(BF16), k_s.astype(BF16)) * decay * incl_ref[...]
        yield

        tinv = eye - lower * level_ref[0]
        for lvl in range(1, level_ref.shape[0]):
            t_b = tinv.astype(BF16)
            half = _dot(t_b, (lower * level_ref[lvl]).astype(BF16)).astype(BF16)
            yield
            tinv = tinv - _dot(half, t_b)
            yield
        resid = eye - _dot3(eye + lower, tinv)
        yield
        tinv = tinv + _dot(tinv.astype(BF16), resid.astype(BF16))
        yield

        rhs = jnp.concatenate([v_s * beta, kb_s * egc], axis=1).astype(BF16)
        sol = _dot_exact_rhs(tinv, rhs)
        u, w = sol[:, 0:BRANCH_WIDTH], sol[:, BRANCH_WIDTH:]
        q_in = q_s * egc
        k_up = k_s * jnp.exp(g_last - gc)
        yield

        state = state_ref[b]
        state_b = state.astype(BF16)
        v_new = u - _dot(w.astype(BF16), state_b)
        yield
        o_s = _dot(q_in.astype(BF16), state_b) + _dot(scores.astype(BF16), v_new.astype(BF16))
        state_ref[b] = jnp.exp(g_last) * state + _dot_tn(k_up.astype(BF16), v_new.astype(BF16))
        yield

        o = o_s[0:c]
        for h in range(1, N_HEADS):
            o = o + o_s[h * c:(h + 1) * c]
        gate = x_ref[b, :, qkv_w:qkv_w + BRANCH_WIDTH]
        o_ref[b] = (_head_rms(o, g_ref[...], head_ones) * _silu(gate)).astype(o_ref.dtype)

    for _ in itertools.zip_longest(*[chunk_of_row(b) for b in range(x_ref.shape[0])]):
        pass


def _dn_masks():
    n = N_HEADS * DN_CHUNK
    r = jnp.arange(n)[:, None]
    cidx = jnp.arange(n)[None, :]
    same_head = r // DN_CHUNK == cidx // DN_CHUNK
    incl = (same_head & (cidx <= r)).astype(F32)
    strict = (same_head & (cidx < r)).astype(F32)
    levels = []
    s = 1
    while s < DN_CHUNK:
        levels.append(((r // (2 * s) == cidx // (2 * s)) & (r % (2 * s) >= s) & (cidx % (2 * s) < s)).astype(F32))
        s *= 2
    return incl, strict, jnp.stack(levels)


def _deltanet(dn3, conv_w_t, a_row, dt_row, g_row):
    b, s, _ = dn3.shape
    c = DN_CHUNK
    n = N_HEADS * c
    incl, strict, levels = _dn_masks()
    assert c == HEAD_DIM
    return pl.pallas_call(
        _dn_body,
        grid=(s // c,),
        in_specs=[pl.BlockSpec((b, c, DN_W), lambda i: (0, i, 0)),
                  _resident((DN_CONV, 3 * BRANCH_WIDTH)), _resident((1, 128)), _resident((1, 128)),
                  _resident((1, BRANCH_WIDTH)), _resident((n, n)), _resident((n, n)), _resident(levels.shape)],
        out_specs=pl.BlockSpec((b, c, BRANCH_WIDTH), lambda i: (0, i, 0)),
        out_shape=jax.ShapeDtypeStruct((b, s, BRANCH_WIDTH), BF16),
        scratch_shapes=[pltpu.VMEM((b, c + 8, 3 * BRANCH_WIDTH), F32), pltpu.VMEM((b, n, BRANCH_WIDTH), F32)],
        compiler_params=_params("arbitrary"),
        name="deltanet",
    )(dn3, conv_w_t, a_row, dt_row, g_row, incl, strict, levels)


def _ld_halves(ref, rows):
    return jnp.concatenate([ref[0, rows, :], ref[1, rows, :]], axis=1)


def _st_halves(ref, rows, val):
    ref[0, rows, :] = val[:, 0:128]
    ref[1, rows, :] = val[:, 128:256]


def _dil_body(q_ref, kc_ref, kp_ref, vc_ref, vp_ref, o_ref, qx_ref, kx_ref, vx_ref, m_ref, l_ref, acc_ref):
    n = pl.program_id(1)
    sup = DIL_SUPER
    blk = DIL_BLOCK
    _st_halves(qx_ref, slice(0, sup), q_ref[...])
    _st_halves(kx_ref, slice(0, sup), kp_ref[...])
    _st_halves(kx_ref, slice(sup, 2 * sup), kc_ref[...])
    _st_halves(vx_ref, slice(0, sup), vp_ref[...])
    _st_halves(vx_ref, slice(sup, 2 * sup), vc_ref[...])

    head_of_lane = _iota((1, BRANCH_WIDTH), 1) // HEAD_DIM
    qi = _iota((N_HEADS * blk, 2 * blk), 0) % blk
    kj = _iota((N_HEADS * blk, 2 * blk), 1)
    dist = qi + blk - kj
    band = (dist >= 0) & (dist <= DIL_KEYS)
    in_current = kj >= blk

    def stack(t):
        return jnp.concatenate([jnp.where(head_of_lane == h, t, jnp.zeros_like(t)) for h in range(N_HEADS)], axis=0)

    def per_head_lanes(col):
        out = jnp.broadcast_to(col[0:blk], (blk, BRANCH_WIDTH))
        for h in range(1, N_HEADS):
            out = jnp.where(head_of_lane == h, col[h * blk:(h + 1) * blk], out)
        return out

    def query_block(qb, p, dil):
        r = qb % dil
        m = qb // dil
        q_start = r + dil * blk * m
        k_start = sup + r + dil * blk * (m - 1)
        if dil == 1:
            q_rows = pl.ds(q_start, blk)
            k_rows = pl.ds(k_start, 2 * blk)
        else:
            q_rows = pl.ds(q_start, blk, stride=dil)
            k_rows = pl.ds(k_start, 2 * blk, stride=dil)
        q = stack(_ld_halves(qx_ref, q_rows).astype(BF16))
        k = _ld_halves(kx_ref, k_rows).astype(BF16)
        v = stack(_ld_halves(vx_ref, k_rows).astype(BF16))
        s = _dot_nt(q, k)
        yield
        has_prev = jnp.logical_or(n > 0, m > 0)
        s = jnp.where(band & jnp.logical_or(in_current, has_prev), s, -jnp.inf)
        mx = jnp.max(s, axis=-1, keepdims=True)
        pr = jnp.exp(s - mx)
        den = jnp.sum(pr, axis=-1, keepdims=True)
        pr = pr.astype(BF16)
        num = _dot(jnp.concatenate([pr[h * blk:(h + 1) * blk] for h in range(N_HEADS)], axis=1), v)
        yield
        mx_t = per_head_lanes(mx)
        den_t = per_head_lanes(den)
        if p == 0:
            _st_halves(m_ref, q_rows, mx_t)
            _st_halves(l_ref, q_rows, den_t)
            _st_halves(acc_ref, q_rows, num)
        else:
            m_old = _ld_halves(m_ref, q_rows)
            m_new = jnp.maximum(m_old, mx_t)
            a_old = jnp.exp(m_old - m_new)
            a_new = jnp.exp(mx_t - m_new)
            _st_halves(m_ref, q_rows, m_new)
            _st_halves(l_ref, q_rows, a_old * _ld_halves(l_ref, q_rows) + a_new * den_t)
            _st_halves(acc_ref, q_rows, a_old * _ld_halves(acc_ref, q_rows) + a_new * num)

    for p, dil in enumerate(DIL_DILATIONS):
        def blocks(i, carry, p=p, dil=dil):
            group = [query_block(DIL_GROUP * i + g, p, dil) for g in range(DIL_GROUP)]
            for _ in itertools.zip_longest(*group):
                pass
            return carry

        lax.fori_loop(0, sup // blk // DIL_GROUP, blocks, 0)

    whole = slice(0, sup)
    o_ref[...] = (_ld_halves(acc_ref, whole) / _ld_halves(l_ref, whole)).astype(o_ref.dtype)


def _dilated(dil3):
    b, s, _ = dil3.shape
    sup = DIL_SUPER
    cur = lambda blk: pl.BlockSpec((None, sup, BRANCH_WIDTH), lambda bi, i: (bi, i, blk))
    prev = lambda blk: pl.BlockSpec((None, sup, BRANCH_WIDTH), lambda bi, i: (bi, jnp.maximum(i - 1, 0), blk))
    return pl.pallas_call(
        _dil_body,
        grid=(b, s // sup),
        in_specs=[cur(0), cur(1), prev(1), cur(2), prev(2)],
        out_specs=pl.BlockSpec((None, sup, BRANCH_WIDTH), lambda bi, i: (bi, i, 0)),
        out_shape=jax.ShapeDtypeStruct((b, s, BRANCH_WIDTH), BF16),
        scratch_shapes=[pltpu.VMEM((2, sup, 128), F32), pltpu.VMEM((2, 2 * sup, 128), F32),
                        pltpu.VMEM((2, 2 * sup, 128), F32), pltpu.VMEM((2, sup, 128), F32),
                        pltpu.VMEM((2, sup, 128), F32), pltpu.VMEM((2, sup, 128), F32)],
        compiler_params=_params("parallel", "arbitrary"),
        name="dilated",
    )(dil3, dil3, dil3, dil3, dil3)


def _merge_body(x_ref, g_ref, wg_ref, o0_ref, o1_ref, o2_ref, o3_ref, wb_ref, wo_ref, out_ref):
    x = x_ref[...]
    d_model = x.shape[-1]
    h = (x * lax.rsqrt(jnp.mean(x * x, axis=-1, keepdims=True) + EPS) * g_ref[...]).astype(BF16)
    merged = None
    for nb, o_ref in enumerate((o0_ref, o1_ref, o2_ref, o3_ref)):
        gate = jax.nn.sigmoid(_dot(h, wg_ref[:, nb * d_model:(nb + 1) * d_model]))
        term = gate * _dot(o_ref[...], wb_ref[nb])
        merged = term if merged is None else merged + term
    out_ref[...] = x + _dot(merged.astype(BF16), wo_ref[...])


def _merge(x2, g_row, w_gate, branches, w_branch, w_out):
    t, d_model = x2.shape
    tm = ROW_TILE
    row = lambda w: pl.BlockSpec((tm, w), lambda i: (i, 0))
    return pl.pallas_call(
        _merge_body,
        grid=(t // tm,),
        in_specs=[row(d_model), _resident((1, d_model)), _resident((d_model, N_BRANCH * d_model))]
        + [row(BRANCH_WIDTH)] * N_BRANCH
        + [_resident((N_BRANCH, BRANCH_WIDTH, d_model)), _resident((d_model, d_model))],
        out_specs=row(d_model),
        out_shape=jax.ShapeDtypeStruct((t, d_model), F32),
        compiler_params=_params("parallel"),
        name="merge_out",
    )(x2, g_row, w_gate, *branches, w_branch, w_out)


def _mlp_body(x_ref, g_ref, w1_ref, w2_ref, out_ref):
    x = x_ref[...]
    h = (x * lax.rsqrt(jnp.mean(x * x, axis=-1, keepdims=True) + EPS) * g_ref[...]).astype(BF16)
    acc = x
    for c0 in range(0, w1_ref.shape[1], FF_TILE):
        a = jnp.maximum(_dot(h, w1_ref[:, c0:c0 + FF_TILE]), 0.0)
        acc = acc + _dot((a * a).astype(BF16), w2_ref[c0:c0 + FF_TILE, :])
    out_ref[...] = acc


def _mlp(x2, g_row, w1, w2):
    t, d_model = x2.shape
    d_ff = w1.shape[1]
    tm = ROW_TILE
    row = pl.BlockSpec((tm, d_model), lambda i: (i, 0))
    return pl.pallas_call(
        _mlp_body,
        grid=(t // tm,),
        in_specs=[row, _resident((1, d_model)), _resident((d_model, d_ff)), _resident((d_ff, d_model))],
        out_specs=row,
        out_shape=jax.ShapeDtypeStruct((t, d_model), F32),
        compiler_params=_params("parallel"),
        name="mlp",
    )(x2, g_row, w1, w2)


def _mixer_weights(w_in):
    d_model = w_in.shape[0]
    bw, kw = BRANCH_WIDTH, GLA_KEY_WIDTH
    o = 0
    sb = w_in[:, o:o + 3 * bw]
    o += 3 * bw
    gla_q, gla_k = w_in[:, o:o + kw], w_in[:, o + kw:o + 2 * kw]
    o += 2 * kw
    gla_v = w_in[:, o:o + bw]
    o += bw
    gla_lr = w_in[:, o:o + GLA_LOW_RANK]
    o += GLA_LOW_RANK
    gla_r = w_in[:, o:o + bw]
    o += bw
    dn_qkv = w_in[:, o:o + 3 * bw]
    o += 3 * bw
    dn_ab = w_in[:, o:o + 2 * N_HEADS]
    o += 2 * N_HEADS
    dn_gate = w_in[:, o:o + bw]
    o += bw
    dil = w_in[:, o:o + 3 * bw]
    o += 3 * bw
    zeros = lambda n: jnp.zeros((d_model, n), w_in.dtype)
    gla = jnp.concatenate([gla_q, gla_k, gla_v, gla_r, gla_lr, zeros(GLA_W - 3 * bw - GLA_LOW_RANK)], axis=1)
    dn = jnp.concatenate([dn_qkv, dn_gate, dn_ab, zeros(DN_W - 4 * bw - 2 * N_HEADS)], axis=1)
    return jnp.concatenate([sb, gla, dn, dil], axis=1).astype(BF16), o


def _rope_tables(seq):
    half = ROT_DIM // 2
    inv_freq = ROPE_THETA ** (-jnp.arange(half, dtype=F32) / half)
    ang = jnp.arange(seq, dtype=jnp.int32).astype(F32)[:, None] * inv_freq[None, :]
    cos, sin = jnp.cos(ang), jnp.sin(ang)
    rest = HEAD_DIM - ROT_DIM
    cos_h = jnp.concatenate([cos, cos, jnp.ones((seq, rest), F32)], axis=1)
    sin_h = jnp.concatenate([-sin, sin, jnp.zeros((seq, rest), F32)], axis=1)
    return jnp.tile(cos_h, (1, N_HEADS)), jnp.tile(sin_h, (1, N_HEADS))


def kernel(x, norm_mix_g, norm_mlp_g, w_in, gla_w_lr2, gla_b_lr, gla_norm_g, dn_conv_w, dn_a_log, dn_dt_bias,
           dn_norm_g, dil_q_norm_g, dil_k_norm_g, w_branch, w_out, w_mlp_in, w_mlp_out):
    b, s, d_model = x.shape
    depth = w_in.shape[0]
    t = b * s
    assert t % ROW_TILE == 0 and s % ROW_TILE == 0 and s % DIL_SUPER == 0 and s % SB_TILE == 0
    cos_t, sin_t = _rope_tables(s)
    tile_heads = lambda g: jnp.tile(g, N_HEADS)[None, :]
    pad_lanes = lambda v: jnp.pad(v, (0, 128 - v.shape[0]))[None, :]

    x2 = x.reshape(t, d_model)
    for l in range(depth):
        w_mix, n_mix = _mixer_weights(w_in[l])
        w_gate = w_in[l][:, n_mix:].astype(BF16)
        sb, gla, dn, dil = _in_proj(x2, norm_mix_g[l][None, :], w_mix, cos_t, sin_t,
                                    tile_heads(dil_q_norm_g[l]), tile_heads(dil_k_norm_g[l]), s)
        o_sb = _stick_breaking(sb.reshape(b, s, SB_W))
        w2_pad = jnp.pad(gla_w_lr2[l], ((0, GLA_KEY_WIDTH - GLA_LOW_RANK), (0, 0)))
        o_gla = _gla(gla.reshape(b, s, GLA_W), w2_pad, gla_b_lr[l][None, :], tile_heads(gla_norm_g[l]))
        o_dn = _deltanet(dn.reshape(b, s, DN_W), dn_conv_w[l].T, pad_lanes(-jnp.exp(dn_a_log[l])),
                         pad_lanes(dn_dt_bias[l]), tile_heads(dn_norm_g[l]))
        o_dil = _dilated(dil.reshape(b, s, DIL_W))
        branches = [o.reshape(t, BRANCH_WIDTH) for o in (o_sb, o_gla, o_dn, o_dil)]
        x2 = _merge(x2, norm_mix_g[l][None, :], w_gate, branches, w_branch[l].astype(BF16), w_out[l].astype(BF16))
        x2 = _mlp(x2, norm_mlp_g[l][None, :], w_mlp_in[l].astype(BF16), w_mlp_out[l].astype(BF16))
    return x2.reshape(b, s, d_model)
```

```python
import itertools

import jax
import jax.numpy as jnp
from jax import lax
from jax.experimental import pallas as pl
from jax.experimental.pallas import tpu as pltpu

F32 = jnp.float32
BF16 = jnp.bfloat16

HEAD_DIM = 64
N_HEADS = 4
BRANCH_WIDTH = N_HEADS * HEAD_DIM
GLA_KEY_DIM = 32
GLA_LOW_RANK = 16
GLA_GATE_NORM = 16.0
GLA_KEY_WIDTH = N_HEADS * GLA_KEY_DIM
DN_CONV = 4
DIL_DILATIONS = (1, 4, 16)
DIL_KEYS = 128
ROPE_THETA = 500000.0
ROT_DIM = HEAD_DIM // 4
N_BRANCH = 4
EPS = 1e-6
LOG2_E = 1.4426950408889634

SB_W = 3 * BRANCH_WIDTH
GLA_W = 896
DN_W = 1152
DIL_W = 3 * BRANCH_WIDTH
MIX_W = SB_W + GLA_W + DN_W + DIL_W

ROW_TILE = 512
SB_TILE = 256
SB_PAIRS = 2
GLA_CHUNK = 64
GLA_SUB = 16
GLA_STEP = 128
DN_CHUNK = 64
DN_STEP = 128
DIL_BLOCK = 128
DIL_SUPER = DIL_BLOCK * DIL_DILATIONS[-1]
DIL_GROUP = 2
FF_TILE = 1024

V7X_VMEM_LIMIT = 56 * 1024 * 1024


def _dot(a, b):
    return jnp.dot(a, b, preferred_element_type=F32)


def _dot_nt(a, b):
    return lax.dot_general(a, b, (((1,), (1,)), ((), ())), preferred_element_type=F32)


def _dot_tn(a, b):
    return lax.dot_general(a, b, (((0,), (0,)), ((), ())), preferred_element_type=F32)


def _split2(a):
    hi = a.astype(BF16)
    lo = (a - hi.astype(F32)).astype(BF16)
    return hi, lo


def _split3(a):
    hi = a.astype(BF16)
    r = a - hi.astype(F32)
    mid = r.astype(BF16)
    lo = (r - mid.astype(F32)).astype(BF16)
    return hi, mid, lo


def _dot3(a, b, dot=_dot):
    ah, al = _split2(a)
    bh, bl = _split2(b)
    return dot(ah, bh) + (dot(ah, bl) + dot(al, bh))


def _dot_exact_rhs(a, b_exact, dot=_dot, parts=2):
    pieces = _split2(a) if parts == 2 else _split3(a)
    out = dot(pieces[0], b_exact)
    for p in pieces[1:]:
        out = out + dot(p, b_exact)
    return out


def _iota(shape, dim):
    return lax.broadcasted_iota(jnp.int32, shape, dim)


def _group_ones(rows, cols, rgroup, cgroup):
    return (_iota((rows, cols), 0) // rgroup == _iota((rows, cols), 1) // cgroup).astype(BF16)


def _log_sigmoid(z):
    return jnp.minimum(z, 0.0) - jnp.log1p(jnp.exp(-jnp.abs(z)))


def _softplus(z):
    return jnp.maximum(z, 0.0) + jnp.log1p(jnp.exp(-jnp.abs(z)))


def _silu(z):
    return z * jax.nn.sigmoid(z)


def _head_rms(x, gain_row, head_ones):
    ms = _dot_exact_rhs(x * x, head_ones) * (1.0 / HEAD_DIM)
    return x * lax.rsqrt(ms + EPS) * gain_row


def _params(*semantics):
    return pltpu.CompilerParams(dimension_semantics=semantics, vmem_limit_bytes=V7X_VMEM_LIMIT)


def _resident(shape):
    nd = len(shape)
    return pl.BlockSpec(shape, lambda *_: (0,) * nd, pipeline_mode=pl.Buffered(1))


def _in_proj_body(x_ref, g_ref, w_ref, cos_ref, sin_ref, qg_ref, kg_ref, sb_ref, gla_ref, dn_ref, dil_ref):
    x = x_ref[...]
    h = (x * lax.rsqrt(jnp.mean(x * x, axis=-1, keepdims=True) + EPS) * g_ref[...]).astype(BF16)
    q_scale = jnp.where(_iota((1, SB_W), 1) < BRANCH_WIDTH, LOG2_E * HEAD_DIM ** -0.5, 1.0)
    sb_ref[...] = (_dot(h, w_ref[:, 0:SB_W]) * q_scale).astype(BF16)
    off = SB_W
    gla_ref[...] = _dot(h, w_ref[:, off:off + GLA_W])
    off += GLA_W
    dn_ref[...] = _dot(h, w_ref[:, off:off + DN_W])
    off += DN_W
    d = _dot(h, w_ref[:, off:off + DIL_W])

    head_ones = _group_ones(BRANCH_WIDTH, BRANCH_WIDTH, HEAD_DIM, HEAD_DIM)
    lane = _iota((1, BRANCH_WIDTH), 1) % HEAD_DIM
    first_half = lane < ROT_DIM // 2
    cos = cos_ref[...]
    sin = sin_ref[...]

    def norm_rope(t, gain_row):
        y = _head_rms(t, gain_row, head_ones)
        partner = jnp.where(first_half, pltpu.roll(y, BRANCH_WIDTH - ROT_DIM // 2, 1), pltpu.roll(y, ROT_DIM // 2, 1))
        return y * cos + partner * sin

    dil_ref[:, 0:BRANCH_WIDTH] = norm_rope(d[:, 0:BRANCH_WIDTH], qg_ref[...]) * (HEAD_DIM ** -0.5)
    dil_ref[:, BRANCH_WIDTH:2 * BRANCH_WIDTH] = norm_rope(d[:, BRANCH_WIDTH:2 * BRANCH_WIDTH], kg_ref[...])
    dil_ref[:, 2 * BRANCH_WIDTH:] = d[:, 2 * BRANCH_WIDTH:]


def _in_proj(x2, g_row, w_mix, cos_t, sin_t, qg_row, kg_row, seq):
    t, d_model = x2.shape
    tm = ROW_TILE
    n_seq_tiles = seq // tm
    row = lambda w: pl.BlockSpec((tm, w), lambda i: (i, 0))
    tab = pl.BlockSpec((tm, BRANCH_WIDTH), lambda i: (i % n_seq_tiles, 0))
    return pl.pallas_call(
        _in_proj_body,
        grid=(t // tm,),
        in_specs=[row(d_model), _resident((1, d_model)), _resident((d_model, MIX_W)), tab, tab,
                  _resident((1, BRANCH_WIDTH)), _resident((1, BRANCH_WIDTH))],
        out_specs=[row(SB_W), row(GLA_W), row(DN_W), row(DIL_W)],
        out_shape=[jax.ShapeDtypeStruct((t, SB_W), BF16), jax.ShapeDtypeStruct((t, GLA_W), F32),
                   jax.ShapeDtypeStruct((t, DN_W), F32), jax.ShapeDtypeStruct((t, DIL_W), F32)],
        compiler_params=_params("parallel"),
        name="in_proj",
    )(x2, g_row, w_mix, cos_t, sin_t, qg_row, kg_row)


def _sb_body(q_ref, k_ref, v_ref, o_ref, z_a, z_b, w_a, w_b):
    tq = q_ref.shape[0]
    qi = pl.program_id(1)
    q = q_ref[...]
    head_of_lane = _iota((1, BRANCH_WIDTH), 1) // HEAD_DIM
    q_stack = jnp.concatenate([jnp.where(head_of_lane == h, q, jnp.zeros_like(q)) for h in range(N_HEADS)], axis=0)
    rows = _iota((tq, tq), 0)
    cols = _iota((tq, tq), 1)
    from_here = (rows >= cols).astype(BF16)
    causal = jnp.concatenate([cols < rows] * N_HEADS, axis=0)
    sign_bit = jnp.uint32(0x80000000)

    def scores(j):
        return _dot_nt(q_stack, k_ref[pl.ds(pl.multiple_of(j * tq, tq), tq), :])

    def weights(z_ref, run, masked):
        z = z_ref[...]
        neg_abs = lax.bitcast_convert_type(lax.bitcast_convert_type(z, jnp.uint32) | sign_bit, F32)
        sp = jnp.maximum(z, 0.0) + jnp.log(1.0 + jnp.exp2(neg_abs)) * LOG2_E
        if masked:
            sp = jnp.where(causal, sp, 0.0)
        tail = _dot(sp.astype(BF16), from_here)
        w = jnp.exp2((z_ref[...] - run) - tail)
        if masked:
            w = jnp.where(causal, w, 0.0)
        w = w.astype(BF16)
        w_cat = jnp.concatenate([w[h * tq:(h + 1) * tq] for h in range(N_HEADS)], axis=1)
        return w_cat, run + tail[:, 0:1]

    def apply(w_cat, j):
        vb = v_ref[pl.ds(pl.multiple_of(j * tq, tq), tq), :]
        v_stack = jnp.concatenate([jnp.where(head_of_lane == h, vb, jnp.zeros_like(vb)) for h in range(N_HEADS)], axis=0)
        return _dot(w_cat, v_stack)

    def step(j, z_cur, z_nxt, w_cur, w_prv, acc, run, masked=False):
        if w_prv is not None:
            acc = acc + apply(w_prv[...], jnp.maximum(j + 1, 0))
        z_nxt[...] = scores(jnp.maximum(j - 1, 0))
        w_cat, run = weights(z_cur, run + jnp.where(j < 0, jnp.inf, 0.0), masked)
        w_cur[...] = w_cat
        return acc, run

    z_b[...] = scores(qi)
    acc, run = step(qi, z_b, z_a, w_b, None, jnp.zeros((tq, BRANCH_WIDTH), F32),
                    jnp.zeros((N_HEADS * tq, 1), F32), masked=True)

    def pairs(first, n_pairs, acc, run):
        for pair in range(n_pairs):
            j = first - 2 * pair
            acc, run = step(j, z_a, z_b, w_a, w_b, acc, run)
            acc, run = step(j - 1, z_b, z_a, w_b, w_a, acc, run)
        return acc, run

    n_main = qi // (2 * SB_PAIRS)
    acc, run = lax.fori_loop(
        0, n_main, lambda u, c: pairs(qi - 1 - 2 * SB_PAIRS * u, SB_PAIRS, *c), (acc, run))
    rest = qi - 1 - 2 * SB_PAIRS * n_main
    acc, run = lax.fori_loop(0, (rest + 2) // 2, lambda u, c: pairs(rest - 2 * u, 1, *c), (acc, run))
    o_ref[...] = (acc + apply(w_b[...], 0)).astype(o_ref.dtype)


def _stick_breaking(sb3):
    b, s, _ = sb3.shape
    tq = SB_TILE
    whole = lambda blk: pl.BlockSpec((None, s, BRANCH_WIDTH), lambda bi, i: (bi, 0, blk), pipeline_mode=pl.Buffered(1))
    return pl.pallas_call(
        _sb_body,
        grid=(b, s // tq),
        in_specs=[pl.BlockSpec((None, tq, BRANCH_WIDTH), lambda bi, i: (bi, i, 0)), whole(1), whole(2)],
        out_specs=pl.BlockSpec((None, tq, BRANCH_WIDTH), lambda bi, i: (bi, i, 0)),
        out_shape=jax.ShapeDtypeStruct((b, s, BRANCH_WIDTH), BF16),
        scratch_shapes=[pltpu.VMEM((N_HEADS * tq, tq), F32), pltpu.VMEM((N_HEADS * tq, tq), F32),
                        pltpu.VMEM((tq, N_HEADS * tq), BF16), pltpu.VMEM((tq, N_HEADS * tq), BF16)],
        compiler_params=_params("parallel", "arbitrary"),
        name="stick_breaking",
    )(sb3, sb3, sb3)


def _gla_body(x_ref, w2_ref, b2_ref, g_ref, o_ref, state_ref):
    c = GLA_CHUNK
    nsub = c // GLA_SUB
    n_rows = x_ref.shape[0]
    n_chunks = GLA_STEP // c

    @pl.when(pl.program_id(0) == 0)
    def _():
        state_ref[...] = jnp.zeros_like(state_ref)

    ri = _iota((c, c), 0)
    ci = _iota((c, c), 1)
    tril_incl = (ci <= ri).astype(BF16)
    earlier_sub = (ci // GLA_SUB) < (ri // GLA_SUB)
    key_head = _iota((1, GLA_KEY_WIDTH), 1) // GLA_KEY_DIM
    val_head = _iota((1, BRANCH_WIDTH), 1) // HEAD_DIM
    key_to_val = _group_ones(GLA_KEY_WIDTH, BRANCH_WIDTH, GLA_KEY_DIM, HEAD_DIM)
    state_mask = key_to_val > 0
    head_ones = _group_ones(BRANCH_WIDTH, BRANCH_WIDTH, HEAD_DIM, HEAD_DIM)
    ones_cv = jnp.ones((c, BRANCH_WIDTH), BF16)
    sub_pos = _iota((1, GLA_SUB, 1), 1)
    scale = GLA_KEY_DIM ** -0.5

    within = {}

    def chunk_without_state(b, ic):
        rows = slice(ic * c, (ic + 1) * c)
        q = x_ref[b, rows, 0:128] * scale
        k = x_ref[b, rows, 128:256]
        v = x_ref[b, rows, 256:512]
        lr = x_ref[b, rows, 768:896]
        zg = jnp.dot(lr, w2_ref[...], precision=lax.Precision.HIGHEST, preferred_element_type=F32) + b2_ref[...]
        yield
        la = _log_sigmoid(zg) * (1.0 / GLA_GATE_NORM)
        bcum = _dot_exact_rhs(la, tril_incl, dot=lambda p, m: _dot(m, p), parts=3)
        btot_kv = _dot_exact_rhs(la, ones_cv, dot=_dot_tn, parts=3)
        yield
        b_last = bcum[c - 1:c, :]
        qe_b = (q * jnp.exp(bcum)).astype(BF16)
        ke_b = (k * jnp.exp(-bcum)).astype(BF16)
        kl = k * jnp.exp(b_last - bcum)
        state_add = jnp.where(state_mask, _dot_tn(kl.astype(BF16), v.astype(BF16)), 0.0)

        scores = [_dot_nt(jnp.where(key_head == h, qe_b, jnp.zeros_like(qe_b)), ke_b) for h in range(N_HEADS)]
        yield
        p_cat = jnp.concatenate([jnp.where(earlier_sub, sc, 0.0).astype(BF16) for sc in scores], axis=1)
        v_stack = jnp.concatenate([jnp.where(val_head == h, v, 0.0).astype(BF16) for h in range(N_HEADS)], axis=0)
        o = _dot(p_cat, v_stack)

        q3 = q.reshape(nsub, GLA_SUB, GLA_KEY_WIDTH)
        k3 = k.reshape(nsub, GLA_SUB, GLA_KEY_WIDTH)
        b3 = bcum.reshape(nsub, GLA_SUB, GLA_KEY_WIDTH)
        v3 = v.reshape(nsub, GLA_SUB, BRANCH_WIDTH)
        for jj in range(GLA_SUB):
            kj = k3[:, jj:jj + 1, :]
            bj = b3[:, jj:jj + 1, :]
            vj = jnp.broadcast_to(v3[:, jj:jj + 1, :], v3.shape).reshape(c, BRANCH_WIDTH)
            dec = jnp.exp(jnp.where(sub_pos >= jj, b3 - bj, -jnp.inf))
            pj = (q3 * kj * dec).reshape(c, GLA_KEY_WIDTH).astype(BF16)
            o = o + _dot(pj, key_to_val) * vj
        within[b, ic] = (qe_b, o, jnp.exp(btot_kv), state_add)

    todo = [chunk_without_state(b, ic) for ic in range(n_chunks) for b in range(n_rows)]
    for _ in itertools.zip_longest(*todo):
        pass

    states = [state_ref[b] for b in range(n_rows)]
    for ic in range(n_chunks):
        for b in range(n_rows):
            qe_b, o, decay_kv, state_add = within[b, ic]
            rows = slice(ic * c, (ic + 1) * c)
            o = o + _dot(qe_b, states[b].astype(BF16))
            states[b] = states[b] * decay_kv + state_add
            y = _head_rms(o, g_ref[...], head_ones) * _silu(x_ref[b, rows, 512:768])
            o_ref[b, rows, :] = y.astype(o_ref.dtype)
    for b in range(n_rows):
        state_ref[b] = states[b]


def _gla(gla3, w2_pad, b2_row, g_row):
    b, s, _ = gla3.shape
    return pl.pallas_call(
        _gla_body,
        grid=(s // GLA_STEP,),
        in_specs=[pl.BlockSpec((b, GLA_STEP, GLA_W), lambda i: (0, i, 0)),
                  _resident((GLA_KEY_WIDTH, GLA_KEY_WIDTH)), _resident((1, GLA_KEY_WIDTH)),
                  _resident((1, BRANCH_WIDTH))],
        out_specs=pl.BlockSpec((b, GLA_STEP, BRANCH_WIDTH), lambda i: (0, i, 0)),
        out_shape=jax.ShapeDtypeStruct((b, s, BRANCH_WIDTH), BF16),
        scratch_shapes=[pltpu.VMEM((b, GLA_KEY_WIDTH, BRANCH_WIDTH), F32)],
        compiler_params=_params("arbitrary"),
        name="gla",
    )(gla3, w2_pad, b2_row, g_row)


def _dn_body(x_ref, cw_ref, a_ref, dt_ref, g_ref, incl_ref, strict_ref, level_ref, o_ref, xbuf_ref, state_ref):
    c = DN_CHUNK
    n = N_HEADS * c
    qkv_w = 3 * BRANCH_WIDTH

    @pl.when(pl.program_id(0) == 0)
    def _():
        state_ref[...] = jnp.zeros_like(state_ref)
        xbuf_ref[:, 0:8, :] = jnp.zeros((x_ref.shape[0], 8, qkv_w), F32)

    head_ones = _group_ones(BRANCH_WIDTH, BRANCH_WIDTH, HEAD_DIM, HEAD_DIM)
    head_of_lane = _iota((1, BRANCH_WIDTH), 1) // HEAD_DIM
    ri = _iota((c, c), 0)
    ci = _iota((c, c), 1)
    cum_incl = (ci <= ri).astype(BF16)
    cum_incl_t = (ri <= ci).astype(BF16)
    eye = (_iota((n, n), 0) == _iota((n, n), 1)).astype(F32)

    def l2n(t):
        return t * lax.rsqrt(_dot_exact_rhs(t * t, head_ones) + EPS)

    def stack(t):
        return jnp.concatenate([jnp.where(head_of_lane == h, t, 0.0) for h in range(N_HEADS)], axis=0)

    def stack_col(t, lane0):
        return jnp.concatenate([t[:, lane0 + h:lane0 + h + 1] for h in range(N_HEADS)], axis=0)

    within = {}

    def chunk_without_state(b, ic):
        rows = slice(ic * c, (ic + 1) * c)
        xbuf_ref[b, 8:8 + c, :] = x_ref[b, rows, 0:qkv_w]
        conv = cw_ref[DN_CONV - 1:DN_CONV, :] * xbuf_ref[b, 8:8 + c, :]
        for i in range(DN_CONV - 1):
            conv = conv + cw_ref[i:i + 1, :] * xbuf_ref[b, pl.ds(8 - (DN_CONV - 1) + i, c), :]
        xbuf_ref[b, 0:8, :] = xbuf_ref[b, c:c + 8, :]
        act = _silu(conv)
        q_all = l2n(act[:, 0:BRANCH_WIDTH]) * (HEAD_DIM ** -0.5)
        k_all = l2n(act[:, BRANCH_WIDTH:2 * BRANCH_WIDTH])
        v_all = act[:, 2 * BRANCH_WIDTH:]

        ab = x_ref[b, rows, qkv_w + BRANCH_WIDTH:]
        g_tok = a_ref[...] * _softplus(ab + dt_ref[...])
        beta_all = jax.nn.sigmoid(ab)
        g_col = _dot_exact_rhs(g_tok, cum_incl, dot=lambda p, m: _dot(m, p), parts=3)
        g_row = _dot_exact_rhs(g_tok, cum_incl_t, dot=_dot_tn, parts=3)
        yield

        gc = stack_col(g_col, 0)
        gr = jnp.concatenate([g_row[h:h + 1, :] for h in range(N_HEADS)], axis=1)
        g_last = jnp.concatenate([jnp.broadcast_to(g_col[c - 1:c, h:h + 1], (c, 1)) for h in range(N_HEADS)], axis=0)
        beta = stack_col(beta_all, N_HEADS)
        egc = jnp.exp(gc)
        q_s, k_s, v_s = stack(q_all), stack(k_all), stack(v_all)
        kb_s = k_s * beta
        decay = jnp.exp(jnp.minimum(gc - gr, 0.0))
        lower = _dot3(kb_s, k_s, dot=_dot_nt) * decay * strict_ref[...]
        scores = _dot_nt(q_s.astype(BF16), k_s.astype(BF16)) * decay * incl_ref[...]
        yield

        tinv = eye - lower * level_ref[0]
        for lvl in range(1, level_ref.shape[0]):
            t_b = tinv.astype(BF16)
            half = _dot(t_b, (lower * level_ref[lvl]).astype(BF16)).astype(BF16)
            yield
            tinv = tinv - _dot(half, t_b)
            yield
        resid = eye - _dot3(eye + lower, tinv)
        yield
        tinv = tinv + _dot(tinv.astype(BF16), resid.astype(BF16))
        yield

        rhs = jnp.concatenate([v_s * beta, kb_s * egc], axis=1).astype(BF16)
        sol = _dot_exact_rhs(tinv, rhs)
        u, w = sol[:, 0:BRANCH_WIDTH], sol[:, BRANCH_WIDTH:]
        q_in = (q_s * egc).astype(BF16)
        k_up = (k_s * jnp.exp(g_last - gc)).astype(BF16)
        within[b, ic] = (u, w.astype(BF16), q_in, k_up, scores.astype(BF16), jnp.exp(g_last))

    n_rows = x_ref.shape[0]
    n_chunks = x_ref.shape[1] // c
    todo = [chunk_without_state(b, ic) for ic in range(n_chunks) for b in range(n_rows)]
    for _ in itertools.zip_longest(*todo):
        pass

    states = [state_ref[b] for b in range(n_rows)]
    for ic in range(n_chunks):
        rows = slice(ic * c, (ic + 1) * c)
        states_b = [s.astype(BF16) for s in states]
        v_new = [within[b, ic][0] - _dot(within[b, ic][1], states_b[b]) for b in range(n_rows)]
        v_new_b = [t.astype(BF16) for t in v_new]
        for b in range(n_rows):
            u, w, q_in, k_up, scores, a_last = within[b, ic]
            o_s = _dot(q_in, states_b[b]) + _dot(scores, v_new_b[b])
            states[b] = a_last * states[b] + _dot_tn(k_up, v_new_b[b])
            o = o_s[0:c]
            for h in range(1, N_HEADS):
                o = o + o_s[h * c:(h + 1) * c]
            gate = x_ref[b, rows, qkv_w:qkv_w + BRANCH_WIDTH]
            o_ref[b, rows, :] = (_head_rms(o, g_ref[...], head_ones) * _silu(gate)).astype(o_ref.dtype)
    for b in range(n_rows):
        state_ref[b] = states[b]


def _dn_masks():
    n = N_HEADS * DN_CHUNK
    r = jnp.arange(n)[:, None]
    cidx = jnp.arange(n)[None, :]
    same_head = r // DN_CHUNK == cidx // DN_CHUNK
    incl = (same_head & (cidx <= r)).astype(F32)
    strict = (same_head & (cidx < r)).astype(F32)
    levels = []
    s = 1
    while s < DN_CHUNK:
        levels.append(((r // (2 * s) == cidx // (2 * s)) & (r % (2 * s) >= s) & (cidx % (2 * s) < s)).astype(F32))
        s *= 2
    return incl, strict, jnp.stack(levels)


def _deltanet(dn3, conv_w_t, a_row, dt_row, g_row):
    b, s, _ = dn3.shape
    c = DN_CHUNK
    n = N_HEADS * c
    incl, strict, levels = _dn_masks()
    assert c == HEAD_DIM
    return pl.pallas_call(
        _dn_body,
        grid=(s // DN_STEP,),
        in_specs=[pl.BlockSpec((b, DN_STEP, DN_W), lambda i: (0, i, 0)),
                  _resident((DN_CONV, 3 * BRANCH_WIDTH)), _resident((1, 128)), _resident((1, 128)),
                  _resident((1, BRANCH_WIDTH)), _resident((n, n)), _resident((n, n)), _resident(levels.shape)],
        out_specs=pl.BlockSpec((b, DN_STEP, BRANCH_WIDTH), lambda i: (0, i, 0)),
        out_shape=jax.ShapeDtypeStruct((b, s, BRANCH_WIDTH), BF16),
        scratch_shapes=[pltpu.VMEM((b, c + 8, 3 * BRANCH_WIDTH), F32), pltpu.VMEM((b, n, BRANCH_WIDTH), F32)],
        compiler_params=_params("arbitrary"),
        name="deltanet",
    )(dn3, conv_w_t, a_row, dt_row, g_row, incl, strict, levels)


def _ld_halves(ref, rows):
    return jnp.concatenate([ref[0, rows, :], ref[1, rows, :]], axis=1)


def _st_halves(ref, rows, val):
    ref[0, rows, :] = val[:, 0:128]
    ref[1, rows, :] = val[:, 128:256]


def _dil_body(q_ref, kc_ref, kp_ref, vc_ref, vp_ref, o_ref, qx_ref, kx_ref, vx_ref, m_ref, l_ref, acc_ref):
    n = pl.program_id(1)
    sup = DIL_SUPER
    blk = DIL_BLOCK
    _st_halves(qx_ref, slice(0, sup), q_ref[...])
    _st_halves(kx_ref, slice(0, sup), kp_ref[...])
    _st_halves(kx_ref, slice(sup, 2 * sup), kc_ref[...])
    _st_halves(vx_ref, slice(0, sup), vp_ref[...])
    _st_halves(vx_ref, slice(sup, 2 * sup), vc_ref[...])

    head_of_lane = _iota((1, BRANCH_WIDTH), 1) // HEAD_DIM
    qi = _iota((N_HEADS * blk, 2 * blk), 0) % blk
    kj = _iota((N_HEADS * blk, 2 * blk), 1)
    dist = qi + blk - kj
    band = (dist >= 0) & (dist <= DIL_KEYS)
    in_current = kj >= blk

    def stack(t):
        return jnp.concatenate([jnp.where(head_of_lane == h, t, jnp.zeros_like(t)) for h in range(N_HEADS)], axis=0)

    def per_head_lanes(col):
        out = jnp.broadcast_to(col[0:blk], (blk, BRANCH_WIDTH))
        for h in range(1, N_HEADS):
            out = jnp.where(head_of_lane == h, col[h * blk:(h + 1) * blk], out)
        return out

    def query_block(qb, p, dil):
        r = qb % dil
        m = qb // dil
        q_start = r + dil * blk * m
        k_start = sup + r + dil * blk * (m - 1)
        if dil == 1:
            q_rows = pl.ds(q_start, blk)
            k_rows = pl.ds(k_start, 2 * blk)
        else:
            q_rows = pl.ds(q_start, blk, stride=dil)
            k_rows = pl.ds(k_start, 2 * blk, stride=dil)
        q = stack(_ld_halves(qx_ref, q_rows).astype(BF16))
        k = _ld_halves(kx_ref, k_rows).astype(BF16)
        v = stack(_ld_halves(vx_ref, k_rows).astype(BF16))
        s = _dot_nt(q, k)
        yield
        has_prev = jnp.logical_or(n > 0, m > 0)
        s = jnp.where(band & jnp.logical_or(in_current, has_prev), s, -jnp.inf)
        mx = jnp.max(s, axis=-1, keepdims=True)
        pr = jnp.exp(s - mx)
        den = jnp.sum(pr, axis=-1, keepdims=True)
        pr = pr.astype(BF16)
        num = _dot(jnp.concatenate([pr[h * blk:(h + 1) * blk] for h in range(N_HEADS)], axis=1), v)
        yield
        mx_t = per_head_lanes(mx)
        den_t = per_head_lanes(den)
        if p == 0:
            _st_halves(m_ref, q_rows, mx_t)
            _st_halves(l_ref, q_rows, den_t)
            _st_halves(acc_ref, q_rows, num)
        else:
            m_old = _ld_halves(m_ref, q_rows)
            m_new = jnp.maximum(m_old, mx_t)
            a_old = jnp.exp(m_old - m_new)
            a_new = jnp.exp(mx_t - m_new)
            _st_halves(m_ref, q_rows, m_new)
            _st_halves(l_ref, q_rows, a_old * _ld_halves(l_ref, q_rows) + a_new * den_t)
            _st_halves(acc_ref, q_rows, a_old * _ld_halves(acc_ref, q_rows) + a_new * num)

    for p, dil in enumerate(DIL_DILATIONS):
        def blocks(i, carry, p=p, dil=dil):
            group = [query_block(DIL_GROUP * i + g, p, dil) for g in range(DIL_GROUP)]
            for _ in itertools.zip_longest(*group):
                pass
            return carry

        lax.fori_loop(0, sup // blk // DIL_GROUP, blocks, 0)

    whole = slice(0, sup)
    o_ref[...] = (_ld_halves(acc_ref, whole) / _ld_halves(l_ref, whole)).astype(o_ref.dtype)


def _dilated(dil3):
    b, s, _ = dil3.shape
    sup = DIL_SUPER
    cur = lambda blk: pl.BlockSpec((None, sup, BRANCH_WIDTH), lambda bi, i: (bi, i, blk))
    prev = lambda blk: pl.BlockSpec((None, sup, BRANCH_WIDTH), lambda bi, i: (bi, jnp.maximum(i - 1, 0), blk))
    return pl.pallas_call(
        _dil_body,
        grid=(b, s // sup),
        in_specs=[cur(0), cur(1), prev(1), cur(2), prev(2)],
        out_specs=pl.BlockSpec((None, sup, BRANCH_WIDTH), lambda bi, i: (bi, i, 0)),
        out_shape=jax.ShapeDtypeStruct((b, s, BRANCH_WIDTH), BF16),
        scratch_shapes=[pltpu.VMEM((2, sup, 128), F32), pltpu.VMEM((2, 2 * sup, 128), F32),
                        pltpu.VMEM((2, 2 * sup, 128), F32), pltpu.VMEM((2, sup, 128), F32),
                        pltpu.VMEM((2, sup, 128), F32), pltpu.VMEM((2, sup, 128), F32)],
        compiler_params=_params("parallel", "arbitrary"),
        name="dilated",
    )(dil3, dil3, dil3, dil3, dil3)


def _merge_body(x_ref, g_ref, wg_ref, o0_ref, o1_ref, o2_ref, o3_ref, wb_ref, wo_ref, out_ref):
    x = x_ref[...]
    d_model = x.shape[-1]
    h = (x * lax.rsqrt(jnp.mean(x * x, axis=-1, keepdims=True) + EPS) * g_ref[...]).astype(BF16)
    merged = None
    for nb, o_ref in enumerate((o0_ref, o1_ref, o2_ref, o3_ref)):
        gate = jax.nn.sigmoid(_dot(h, wg_ref[:, nb * d_model:(nb + 1) * d_model]))
        term = gate * _dot(o_ref[...], wb_ref[nb])
        merged = term if merged is None else merged + term
    out_ref[...] = x + _dot(merged.astype(BF16), wo_ref[...])


def _merge(x2, g_row, w_gate, branches, w_branch, w_out):
    t, d_model = x2.shape
    tm = ROW_TILE
    row = lambda w: pl.BlockSpec((tm, w), lambda i: (i, 0))
    return pl.pallas_call(
        _merge_body,
        grid=(t // tm,),
        in_specs=[row(d_model), _resident((1, d_model)), _resident((d_model, N_BRANCH * d_model))]
        + [row(BRANCH_WIDTH)] * N_BRANCH
        + [_resident((N_BRANCH, BRANCH_WIDTH, d_model)), _resident((d_model, d_model))],
        out_specs=row(d_model),
        out_shape=jax.ShapeDtypeStruct((t, d_model), F32),
        compiler_params=_params("parallel"),
        name="merge_out",
    )(x2, g_row, w_gate, *branches, w_branch, w_out)


def _mlp_body(x_ref, g_ref, w1_ref, w2_ref, out_ref):
    x = x_ref[...]
    h = (x * lax.rsqrt(jnp.mean(x * x, axis=-1, keepdims=True) + EPS) * g_ref[...]).astype(BF16)
    acc = x
    for c0 in range(0, w1_ref.shape[1], FF_TILE):
        a = jnp.maximum(_dot(h, w1_ref[:, c0:c0 + FF_TILE]), 0.0)
        acc = acc + _dot((a * a).astype(BF16), w2_ref[c0:c0 + FF_TILE, :])
    out_ref[...] = acc


def _mlp(x2, g_row, w1, w2):
    t, d_model = x2.shape
    d_ff = w1.shape[1]
    tm = ROW_TILE
    row = pl.BlockSpec((tm, d_model), lambda i: (i, 0))
    return pl.pallas_call(
        _mlp_body,
        grid=(t // tm,),
        in_specs=[row, _resident((1, d_model)), _resident((d_model, d_ff)), _resident((d_ff, d_model))],
        out_specs=row,
        out_shape=jax.ShapeDtypeStruct((t, d_model), F32),
        compiler_params=_params("parallel"),
        name="mlp",
    )(x2, g_row, w1, w2)


def _mixer_weights(w_in):
    d_model = w_in.shape[0]
    bw, kw = BRANCH_WIDTH, GLA_KEY_WIDTH
    o = 0
    sb = w_in[:, o:o + 3 * bw]
    o += 3 * bw
    gla_q, gla_k = w_in[:, o:o + kw], w_in[:, o + kw:o + 2 * kw]
    o += 2 * kw
    gla_v = w_in[:, o:o + bw]
    o += bw
    gla_lr = w_in[:, o:o + GLA_LOW_RANK]
    o += GLA_LOW_RANK
    gla_r = w_in[:, o:o + bw]
    o += bw
    dn_qkv = w_in[:, o:o + 3 * bw]
    o += 3 * bw
    dn_ab = w_in[:, o:o + 2 * N_HEADS]
    o += 2 * N_HEADS
    dn_gate = w_in[:, o:o + bw]
    o += bw
    dil = w_in[:, o:o + 3 * bw]
    o += 3 * bw
    zeros = lambda n: jnp.zeros((d_model, n), w_in.dtype)
    gla = jnp.concatenate([gla_q, gla_k, gla_v, gla_r, gla_lr, zeros(GLA_W - 3 * bw - GLA_LOW_RANK)], axis=1)
    dn = jnp.concatenate([dn_qkv, dn_gate, dn_ab, zeros(DN_W - 4 * bw - 2 * N_HEADS)], axis=1)
    return jnp.concatenate([sb, gla, dn, dil], axis=1).astype(BF16), o


def _rope_tables(seq):
    half = ROT_DIM // 2
    inv_freq = ROPE_THETA ** (-jnp.arange(half, dtype=F32) / half)
    ang = jnp.arange(seq, dtype=jnp.int32).astype(F32)[:, None] * inv_freq[None, :]
    cos, sin = jnp.cos(ang), jnp.sin(ang)
    rest = HEAD_DIM - ROT_DIM
    cos_h = jnp.concatenate([cos, cos, jnp.ones((seq, rest), F32)], axis=1)
    sin_h = jnp.concatenate([-sin, sin, jnp.zeros((seq, rest), F32)], axis=1)
    return jnp.tile(cos_h, (1, N_HEADS)), jnp.tile(sin_h, (1, N_HEADS))


def kernel(x, norm_mix_g, norm_mlp_g, w_in, gla_w_lr2, gla_b_lr, gla_norm_g, dn_conv_w, dn_a_log, dn_dt_bias,
           dn_norm_g, dil_q_norm_g, dil_k_norm_g, w_branch, w_out, w_mlp_in, w_mlp_out):
    b, s, d_model = x.shape
    depth = w_in.shape[0]
    t = b * s
    assert t % ROW_TILE == 0 and s % ROW_TILE == 0 and s % DIL_SUPER == 0 and s % SB_TILE == 0
    cos_t, sin_t = _rope_tables(s)
    tile_heads = lambda g: jnp.tile(g, N_HEADS)[None, :]
    pad_lanes = lambda v: jnp.pad(v, (0, 128 - v.shape[0]))[None, :]

    x2 = x.reshape(t, d_model)
    for l in range(depth):
        w_mix, n_mix = _mixer_weights(w_in[l])
        w_gate = w_in[l][:, n_mix:].astype(BF16)
        sb, gla, dn, dil = _in_proj(x2, norm_mix_g[l][None, :], w_mix, cos_t, sin_t,
                                    tile_heads(dil_q_norm_g[l]), tile_heads(dil_k_norm_g[l]), s)
        o_sb = _stick_breaking(sb.reshape(b, s, SB_W))
        w2_pad = jnp.pad(gla_w_lr2[l], ((0, GLA_KEY_WIDTH - GLA_LOW_RANK), (0, 0)))
        o_gla = _gla(gla.reshape(b, s, GLA_W), w2_pad, gla_b_lr[l][None, :], tile_heads(gla_norm_g[l]))
        o_dn = _deltanet(dn.reshape(b, s, DN_W), dn_conv_w[l].T, pad_lanes(-jnp.exp(dn_a_log[l])),
                         pad_lanes(dn_dt_bias[l]), tile_heads(dn_norm_g[l]))
        o_dil = _dilated(dil.reshape(b, s, DIL_W))
        branches = [o.reshape(t, BRANCH_WIDTH) for o in (o_sb, o_gla, o_dn, o_dil)]
        x2 = _merge(x2, norm_mix_g[l][None, :], w_gate, branches, w_branch[l].astype(BF16), w_out[l].astype(BF16))
        x2 = _mlp(x2, norm_mlp_g[l][None, :], w_mlp_in[l].astype(BF16), w_mlp_out[l].astype(BF16))
    return x2.reshape(b, s, d_model)
```

```python
import itertools

import jax
import jax.numpy as jnp
from jax import lax
from jax.experimental import pallas as pl
from jax.experimental.pallas import tpu as pltpu

F32 = jnp.float32
BF16 = jnp.bfloat16

HEAD_DIM = 64
N_HEADS = 4
BRANCH_WIDTH = N_HEADS * HEAD_DIM
GLA_KEY_DIM = 32
GLA_LOW_RANK = 16
GLA_GATE_NORM = 16.0
GLA_KEY_WIDTH = N_HEADS * GLA_KEY_DIM
DN_CONV = 4
DIL_DILATIONS = (1, 4, 16)
DIL_KEYS = 128
ROPE_THETA = 500000.0
ROT_DIM = HEAD_DIM // 4
N_BRANCH = 4
EPS = 1e-6
LOG2_E = 1.4426950408889634

SB_W = 3 * BRANCH_WIDTH
GLA_W = 896
DN_W = 1152
DIL_W = 3 * BRANCH_WIDTH
MIX_W = SB_W + GLA_W + DN_W + DIL_W

ROW_TILE = 512
SB_TILE = 256
SB_RUN_STOP = 160.0
GLA_CHUNK = 64
GLA_SUB = 16
GLA_STEP = 128
DN_CHUNK = 64
DN_STEP = 128
DIL_BLOCK = 128
DIL_SUPER = DIL_BLOCK * DIL_DILATIONS[-1]
DIL_GROUP = 2
FF_TILE = 1024

V7X_VMEM_LIMIT = 56 * 1024 * 1024


def _dot(a, b):
    return jnp.dot(a, b, preferred_element_type=F32)


def _dot_nt(a, b):
    return lax.dot_general(a, b, (((1,), (1,)), ((), ())), preferred_element_type=F32)


def _dot_tn(a, b):
    return lax.dot_general(a, b, (((0,), (0,)), ((), ())), preferred_element_type=F32)


def _split2(a):
    hi = a.astype(BF16)
    lo = (a - hi.astype(F32)).astype(BF16)
    return hi, lo


def _split3(a):
    hi = a.astype(BF16)
    r = a - hi.astype(F32)
    mid = r.astype(BF16)
    lo = (r - mid.astype(F32)).astype(BF16)
    return hi, mid, lo


def _dot3(a, b, dot=_dot):
    ah, al = _split2(a)
    bh, bl = _split2(b)
    return dot(ah, bh) + (dot(ah, bl) + dot(al, bh))


def _dot_exact_rhs(a, b_exact, dot=_dot, parts=2):
    pieces = _split2(a) if parts == 2 else _split3(a)
    out = dot(pieces[0], b_exact)
    for p in pieces[1:]:
        out = out + dot(p, b_exact)
    return out


def _iota(shape, dim):
    return lax.broadcasted_iota(jnp.int32, shape, dim)


def _group_ones(rows, cols, rgroup, cgroup):
    return (_iota((rows, cols), 0) // rgroup == _iota((rows, cols), 1) // cgroup).astype(BF16)


def _log_sigmoid(z):
    return jnp.minimum(z, 0.0) - jnp.log1p(jnp.exp(-jnp.abs(z)))


def _softplus(z):
    return jnp.maximum(z, 0.0) + jnp.log1p(jnp.exp(-jnp.abs(z)))


def _silu(z):
    return z * jax.nn.sigmoid(z)


def _head_rms(x, gain_row, head_ones):
    ms = _dot_exact_rhs(x * x, head_ones) * (1.0 / HEAD_DIM)
    return x * lax.rsqrt(ms + EPS) * gain_row


def _params(*semantics):
    return pltpu.CompilerParams(dimension_semantics=semantics, vmem_limit_bytes=V7X_VMEM_LIMIT)


def _resident(shape):
    nd = len(shape)
    return pl.BlockSpec(shape, lambda *_: (0,) * nd, pipeline_mode=pl.Buffered(1))


def _in_proj_body(x_ref, g_ref, w_ref, cos_ref, sin_ref, qg_ref, kg_ref, sb_ref, gla_ref, dn_ref, dil_ref):
    x = x_ref[...]
    h = (x * lax.rsqrt(jnp.mean(x * x, axis=-1, keepdims=True) + EPS) * g_ref[...]).astype(BF16)
    q_scale = jnp.where(_iota((1, SB_W), 1) < BRANCH_WIDTH, LOG2_E * HEAD_DIM ** -0.5, 1.0)
    sb_ref[...] = (_dot(h, w_ref[:, 0:SB_W]) * q_scale).astype(BF16)
    off = SB_W
    gla_ref[...] = _dot(h, w_ref[:, off:off + GLA_W])
    off += GLA_W
    dn_ref[...] = _dot(h, w_ref[:, off:off + DN_W])
    off += DN_W
    d = _dot(h, w_ref[:, off:off + DIL_W])

    head_ones = _group_ones(BRANCH_WIDTH, BRANCH_WIDTH, HEAD_DIM, HEAD_DIM)
    lane = _iota((1, BRANCH_WIDTH), 1) % HEAD_DIM
    first_half = lane < ROT_DIM // 2
    cos = cos_ref[...]
    sin = sin_ref[...]

    def norm_rope(t, gain_row):
        y = _head_rms(t, gain_row, head_ones)
        partner = jnp.where(first_half, pltpu.roll(y, BRANCH_WIDTH - ROT_DIM // 2, 1), pltpu.roll(y, ROT_DIM // 2, 1))
        return y * cos + partner * sin

    dil_ref[:, 0:BRANCH_WIDTH] = norm_rope(d[:, 0:BRANCH_WIDTH], qg_ref[...]) * (HEAD_DIM ** -0.5)
    dil_ref[:, BRANCH_WIDTH:2 * BRANCH_WIDTH] = norm_rope(d[:, BRANCH_WIDTH:2 * BRANCH_WIDTH], kg_ref[...])
    dil_ref[:, 2 * BRANCH_WIDTH:] = d[:, 2 * BRANCH_WIDTH:]


def _in_proj(x2, g_row, w_mix, cos_t, sin_t, qg_row, kg_row, seq):
    t, d_model = x2.shape
    tm = ROW_TILE
    n_seq_tiles = seq // tm
    row = lambda w: pl.BlockSpec((tm, w), lambda i: (i, 0))
    tab = pl.BlockSpec((tm, BRANCH_WIDTH), lambda i: (i % n_seq_tiles, 0))
    return pl.pallas_call(
        _in_proj_body,
        grid=(t // tm,),
        in_specs=[row(d_model), _resident((1, d_model)), _resident((d_model, MIX_W)), tab, tab,
                  _resident((1, BRANCH_WIDTH)), _resident((1, BRANCH_WIDTH))],
        out_specs=[row(SB_W), row(GLA_W), row(DN_W), row(DIL_W)],
        out_shape=[jax.ShapeDtypeStruct((t, SB_W), BF16), jax.ShapeDtypeStruct((t, GLA_W), F32),
                   jax.ShapeDtypeStruct((t, DN_W), F32), jax.ShapeDtypeStruct((t, DIL_W), F32)],
        compiler_params=_params("parallel"),
        name="in_proj",
    )(x2, g_row, w_mix, cos_t, sin_t, qg_row, kg_row)


def _sb_body(q_ref, k_ref, v_ref, o_ref, z_a, z_b, w_a, w_b):
    tq = q_ref.shape[0]
    qi = pl.program_id(1)
    q = q_ref[...]
    head_of_lane = _iota((1, BRANCH_WIDTH), 1) // HEAD_DIM
    q_stack = jnp.concatenate([jnp.where(head_of_lane == h, q, jnp.zeros_like(q)) for h in range(N_HEADS)], axis=0)
    rows = _iota((tq, tq), 0)
    cols = _iota((tq, tq), 1)
    from_here = (rows >= cols).astype(BF16)
    causal = jnp.concatenate([cols < rows] * N_HEADS, axis=0)
    sign_bit = jnp.uint32(0x80000000)

    def scores(j):
        return _dot_nt(q_stack, k_ref[pl.ds(pl.multiple_of(j * tq, tq), tq), :])

    def weights(z_ref, run, masked):
        z = z_ref[...]
        neg_abs = lax.bitcast_convert_type(lax.bitcast_convert_type(z, jnp.uint32) | sign_bit, F32)
        sp = jnp.maximum(z, 0.0) + jnp.log(1.0 + jnp.exp2(neg_abs)) * LOG2_E
        if masked:
            sp = jnp.where(causal, sp, 0.0)
        tail = _dot(sp.astype(BF16), from_here)
        w = jnp.exp2((z_ref[...] - run) - tail)
        if masked:
            w = jnp.where(causal, w, 0.0)
        w = w.astype(BF16)
        w_cat = jnp.concatenate([w[h * tq:(h + 1) * tq] for h in range(N_HEADS)], axis=1)
        return w_cat, run + tail[:, 0:1]

    def apply(w_cat, j):
        vb = v_ref[pl.ds(pl.multiple_of(j * tq, tq), tq), :]
        v_stack = jnp.concatenate([jnp.where(head_of_lane == h, vb, jnp.zeros_like(vb)) for h in range(N_HEADS)], axis=0)
        return _dot(w_cat, v_stack)

    def step(j, z_cur, z_nxt, w_cur, w_prv, acc, run, masked=False):
        if w_prv is not None:
            acc = acc + apply(w_prv[...], jnp.maximum(j + 1, 0))
        z_nxt[...] = scores(jnp.maximum(j - 1, 0))
        w_cat, run = weights(z_cur, run + jnp.where(j < 0, jnp.inf, 0.0), masked)
        w_cur[...] = w_cat
        return acc, run

    z_b[...] = scores(qi)
    acc, run = step(qi, z_b, z_a, w_b, None, jnp.zeros((tq, BRANCH_WIDTH), F32),
                    jnp.zeros((N_HEADS * tq, 1), F32), masked=True)
    acc, run = step(qi - 1, z_a, z_b, w_a, w_b, acc, run)

    def more(carry):
        j, _, run = carry
        return jnp.logical_and(j >= 0, jnp.min(run) < SB_RUN_STOP)

    def pair(carry):
        j, acc, run = carry
        acc, run = step(j, z_b, z_a, w_b, w_a, acc, run)
        acc, run = step(j - 1, z_a, z_b, w_a, w_b, acc, run)
        return j - 2, acc, run

    j, acc, run = lax.while_loop(more, pair, (qi - 2, acc, run))
    o_ref[...] = (acc + apply(w_a[...], jnp.maximum(j + 1, 0))).astype(o_ref.dtype)


def _stick_breaking(sb3):
    b, s, _ = sb3.shape
    tq = SB_TILE
    whole = lambda blk: pl.BlockSpec((None, s, BRANCH_WIDTH), lambda bi, i: (bi, 0, blk), pipeline_mode=pl.Buffered(1))
    return pl.pallas_call(
        _sb_body,
        grid=(b, s // tq),
        in_specs=[pl.BlockSpec((None, tq, BRANCH_WIDTH), lambda bi, i: (bi, i, 0)), whole(1), whole(2)],
        out_specs=pl.BlockSpec((None, tq, BRANCH_WIDTH), lambda bi, i: (bi, i, 0)),
        out_shape=jax.ShapeDtypeStruct((b, s, BRANCH_WIDTH), BF16),
        scratch_shapes=[pltpu.VMEM((N_HEADS * tq, tq), F32), pltpu.VMEM((N_HEADS * tq, tq), F32),
                        pltpu.VMEM((tq, N_HEADS * tq), BF16), pltpu.VMEM((tq, N_HEADS * tq), BF16)],
        compiler_params=_params("parallel", "arbitrary"),
        name="stick_breaking",
    )(sb3, sb3, sb3)


def _gla_body(x_ref, w2_ref, b2_ref, g_ref, o_ref, state_ref):
    c = GLA_CHUNK
    nsub = c // GLA_SUB
    n_rows = x_ref.shape[0]
    n_chunks = GLA_STEP // c

    @pl.when(pl.program_id(0) == 0)
    def _():
        state_ref[...] = jnp.zeros_like(state_ref)

    ri = _iota((c, c), 0)
    ci = _iota((c, c), 1)
    tril_incl = (ci <= ri).astype(BF16)
    earlier_sub = (ci // GLA_SUB) < (ri // GLA_SUB)
    key_head = _iota((1, GLA_KEY_WIDTH), 1) // GLA_KEY_DIM
    val_head = _iota((1, BRANCH_WIDTH), 1) // HEAD_DIM
    key_to_val = _group_ones(GLA_KEY_WIDTH, BRANCH_WIDTH, GLA_KEY_DIM, HEAD_DIM)
    state_mask = key_to_val > 0
    head_ones = _group_ones(BRANCH_WIDTH, BRANCH_WIDTH, HEAD_DIM, HEAD_DIM)
    ones_cv = jnp.ones((c, BRANCH_WIDTH), BF16)
    sub_pos = _iota((1, GLA_SUB, 1), 1)
    scale = GLA_KEY_DIM ** -0.5

    within = {}

    def chunk_without_state(b, ic):
        rows = slice(ic * c, (ic + 1) * c)
        q = x_ref[b, rows, 0:128] * scale
        k = x_ref[b, rows, 128:256]
        v = x_ref[b, rows, 256:512]
        lr = x_ref[b, rows, 768:896]
        zg = jnp.dot(lr, w2_ref[...], precision=lax.Precision.HIGHEST, preferred_element_type=F32) + b2_ref[...]
        yield
        la = _log_sigmoid(zg) * (1.0 / GLA_GATE_NORM)
        bcum = _dot_exact_rhs(la, tril_incl, dot=lambda p, m: _dot(m, p), parts=3)
        btot_kv = _dot_exact_rhs(la, ones_cv, dot=_dot_tn, parts=3)
        yield
        b_last = bcum[c - 1:c, :]
        qe_b = (q * jnp.exp(bcum)).astype(BF16)
        ke_b = (k * jnp.exp(-bcum)).astype(BF16)
        kl = k * jnp.exp(b_last - bcum)
        state_add = jnp.where(state_mask, _dot_tn(kl.astype(BF16), v.astype(BF16)), 0.0)

        scores = [_dot_nt(jnp.where(key_head == h, qe_b, jnp.zeros_like(qe_b)), ke_b) for h in range(N_HEADS)]
        yield
        p_cat = jnp.concatenate([jnp.where(earlier_sub, sc, 0.0).astype(BF16) for sc in scores], axis=1)
        v_stack = jnp.concatenate([jnp.where(val_head == h, v, 0.0).astype(BF16) for h in range(N_HEADS)], axis=0)
        o = _dot(p_cat, v_stack)

        q3 = q.reshape(nsub, GLA_SUB, GLA_KEY_WIDTH)
        k3 = k.reshape(nsub, GLA_SUB, GLA_KEY_WIDTH)
        b3 = bcum.reshape(nsub, GLA_SUB, GLA_KEY_WIDTH)
        v3 = v.reshape(nsub, GLA_SUB, BRANCH_WIDTH)
        for jj in range(GLA_SUB):
            kj = k3[:, jj:jj + 1, :]
            bj = b3[:, jj:jj + 1, :]
            vj = jnp.broadcast_to(v3[:, jj:jj + 1, :], v3.shape).reshape(c, BRANCH_WIDTH)
            dec = jnp.exp(jnp.where(sub_pos >= jj, b3 - bj, -jnp.inf))
            pj = (q3 * kj * dec).reshape(c, GLA_KEY_WIDTH).astype(BF16)
            o = o + _dot(pj, key_to_val) * vj
        within[b, ic] = (qe_b, o, jnp.exp(btot_kv), state_add)

    todo = [chunk_without_state(b, ic) for ic in range(n_chunks) for b in range(n_rows)]
    for _ in itertools.zip_longest(*todo):
        pass

    states = [state_ref[b] for b in range(n_rows)]
    for ic in range(n_chunks):
        for b in range(n_rows):
            qe_b, o, decay_kv, state_add = within[b, ic]
            rows = slice(ic * c, (ic + 1) * c)
            o = o + _dot(qe_b, states[b].astype(BF16))
            states[b] = states[b] * decay_kv + state_add
            y = _head_rms(o, g_ref[...], head_ones) * _silu(x_ref[b, rows, 512:768])
            o_ref[b, rows, :] = y.astype(o_ref.dtype)
    for b in range(n_rows):
        state_ref[b] = states[b]


def _gla(gla3, w2_pad, b2_row, g_row):
    b, s, _ = gla3.shape
    return pl.pallas_call(
        _gla_body,
        grid=(s // GLA_STEP,),
        in_specs=[pl.BlockSpec((b, GLA_STEP, GLA_W), lambda i: (0, i, 0)),
                  _resident((GLA_KEY_WIDTH, GLA_KEY_WIDTH)), _resident((1, GLA_KEY_WIDTH)),
                  _resident((1, BRANCH_WIDTH))],
        out_specs=pl.BlockSpec((b, GLA_STEP, BRANCH_WIDTH), lambda i: (0, i, 0)),
        out_shape=jax.ShapeDtypeStruct((b, s, BRANCH_WIDTH), BF16),
        scratch_shapes=[pltpu.VMEM((b, GLA_KEY_WIDTH, BRANCH_WIDTH), F32)],
        compiler_params=_params("arbitrary"),
        name="gla",
    )(gla3, w2_pad, b2_row, g_row)


def _dn_body(x_ref, cw_ref, a_ref, dt_ref, g_ref, incl_ref, strict_ref, level_ref, o_ref, xbuf_ref, state_ref):
    c = DN_CHUNK
    n = N_HEADS * c
    qkv_w = 3 * BRANCH_WIDTH

    @pl.when(pl.program_id(0) == 0)
    def _():
        state_ref[...] = jnp.zeros_like(state_ref)
        xbuf_ref[:, 0:8, :] = jnp.zeros((x_ref.shape[0], 8, qkv_w), F32)

    head_ones = _group_ones(BRANCH_WIDTH, BRANCH_WIDTH, HEAD_DIM, HEAD_DIM)
    head_of_lane = _iota((1, BRANCH_WIDTH), 1) // HEAD_DIM
    ri = _iota((c, c), 0)
    ci = _iota((c, c), 1)
    cum_incl = (ci <= ri).astype(BF16)
    cum_incl_t = (ri <= ci).astype(BF16)
    eye = (_iota((n, n), 0) == _iota((n, n), 1)).astype(F32)

    def l2n(t):
        return t * lax.rsqrt(_dot_exact_rhs(t * t, head_ones) + EPS)

    def stack(t):
        return jnp.concatenate([jnp.where(head_of_lane == h, t, 0.0) for h in range(N_HEADS)], axis=0)

    def stack_col(t, lane0):
        return jnp.concatenate([t[:, lane0 + h:lane0 + h + 1] for h in range(N_HEADS)], axis=0)

    within = {}

    def chunk_without_state(b, ic):
        rows = slice(ic * c, (ic + 1) * c)
        xbuf_ref[b, 8:8 + c, :] = x_ref[b, rows, 0:qkv_w]
        conv = cw_ref[DN_CONV - 1:DN_CONV, :] * xbuf_ref[b, 8:8 + c, :]
        for i in range(DN_CONV - 1):
            conv = conv + cw_ref[i:i + 1, :] * xbuf_ref[b, pl.ds(8 - (DN_CONV - 1) + i, c), :]
        xbuf_ref[b, 0:8, :] = xbuf_ref[b, c:c + 8, :]
        act = _silu(conv)
        q_all = l2n(act[:, 0:BRANCH_WIDTH]) * (HEAD_DIM ** -0.5)
        k_all = l2n(act[:, BRANCH_WIDTH:2 * BRANCH_WIDTH])
        v_all = act[:, 2 * BRANCH_WIDTH:]

        ab = x_ref[b, rows, qkv_w + BRANCH_WIDTH:]
        g_tok = a_ref[...] * _softplus(ab + dt_ref[...])
        beta_all = jax.nn.sigmoid(ab)
        g_col = _dot_exact_rhs(g_tok, cum_incl, dot=lambda p, m: _dot(m, p), parts=3)
        g_row = _dot_exact_rhs(g_tok, cum_incl_t, dot=_dot_tn, parts=3)
        yield

        gc = stack_col(g_col, 0)
        gr = jnp.concatenate([g_row[h:h + 1, :] for h in range(N_HEADS)], axis=1)
        g_last = jnp.concatenate([jnp.broadcast_to(g_col[c - 1:c, h:h + 1], (c, 1)) for h in range(N_HEADS)], axis=0)
        beta = stack_col(beta_all, N_HEADS)
        egc = jnp.exp(gc)
        q_s, k_s, v_s = stack(q_all), stack(k_all), stack(v_all)
        kb_s = k_s * beta
        decay = jnp.exp(jnp.minimum(gc - gr, 0.0))
        lower = _dot3(kb_s, k_s, dot=_dot_nt) * decay * strict_ref[...]
        scores = _dot_nt(q_s.astype(BF16), k_s.astype(BF16)) * decay * incl_ref[...]
        yield

        tinv = eye - lower * level_ref[0]
        for lvl in range(1, level_ref.shape[0]):
            t_b = tinv.astype(BF16)
            half = _dot(t_b, (lower * level_ref[lvl]).astype(BF16)).astype(BF16)
            yield
            tinv = tinv - _dot(half, t_b)
            yield
        resid = eye - _dot3(eye + lower, tinv)
        yield
        tinv = tinv + _dot(tinv.astype(BF16), resid.astype(BF16))
        yield

        rhs = jnp.concatenate([v_s * beta, kb_s * egc], axis=1).astype(BF16)
        sol = _dot_exact_rhs(tinv, rhs)
        u, w = sol[:, 0:BRANCH_WIDTH], sol[:, BRANCH_WIDTH:]
        q_in = (q_s * egc).astype(BF16)
        k_up = (k_s * jnp.exp(g_last - gc)).astype(BF16)
        within[b, ic] = (u, w.astype(BF16), q_in, k_up, scores.astype(BF16), jnp.exp(g_last))

    n_rows = x_ref.shape[0]
    n_chunks = x_ref.shape[1] // c
    todo = [chunk_without_state(b, ic) for ic in range(n_chunks) for b in range(n_rows)]
    for _ in itertools.zip_longest(*todo):
        pass

    states = [state_ref[b] for b in range(n_rows)]
    for ic in range(n_chunks):
        rows = slice(ic * c, (ic + 1) * c)
        states_b = [s.astype(BF16) for s in states]
        v_new = [within[b, ic][0] - _dot(within[b, ic][1], states_b[b]) for b in range(n_rows)]
        v_new_b = [t.astype(BF16) for t in v_new]
        for b in range(n_rows):
            u, w, q_in, k_up, scores, a_last = within[b, ic]
            o_s = _dot(q_in, states_b[b]) + _dot(scores, v_new_b[b])
            states[b] = a_last * states[b] + _dot_tn(k_up, v_new_b[b])
            o = o_s[0:c]
            for h in range(1, N_HEADS):
                o = o + o_s[h * c:(h + 1) * c]
            gate = x_ref[b, rows, qkv_w:qkv_w + BRANCH_WIDTH]
            o_ref[b, rows, :] = (_head_rms(o, g_ref[...], head_ones) * _silu(gate)).astype(o_ref.dtype)
    for b in range(n_rows):
        state_ref[b] = states[b]


def _dn_masks():
    n = N_HEADS * DN_CHUNK
    r = jnp.arange(n)[:, None]
    cidx = jnp.arange(n)[None, :]
    same_head = r // DN_CHUNK == cidx // DN_CHUNK
    incl = (same_head & (cidx <= r)).astype(F32)
    strict = (same_head & (cidx < r)).astype(F32)
    levels = []
    s = 1
    while s < DN_CHUNK:
        levels.append(((r // (2 * s) == cidx // (2 * s)) & (r % (2 * s) >= s) & (cidx % (2 * s) < s)).astype(F32))
        s *= 2
    return incl, strict, jnp.stack(levels)


def _deltanet(dn3, conv_w_t, a_row, dt_row, g_row):
    b, s, _ = dn3.shape
    c = DN_CHUNK
    n = N_HEADS * c
    incl, strict, levels = _dn_masks()
    assert c == HEAD_DIM
    return pl.pallas_call(
        _dn_body,
        grid=(s // DN_STEP,),
        in_specs=[pl.BlockSpec((b, DN_STEP, DN_W), lambda i: (0, i, 0)),
                  _resident((DN_CONV, 3 * BRANCH_WIDTH)), _resident((1, 128)), _resident((1, 128)),
                  _resident((1, BRANCH_WIDTH)), _resident((n, n)), _resident((n, n)), _resident(levels.shape)],
        out_specs=pl.BlockSpec((b, DN_STEP, BRANCH_WIDTH), lambda i: (0, i, 0)),
        out_shape=jax.ShapeDtypeStruct((b, s, BRANCH_WIDTH), BF16),
        scratch_shapes=[pltpu.VMEM((b, c + 8, 3 * BRANCH_WIDTH), F32), pltpu.VMEM((b, n, BRANCH_WIDTH), F32)],
        compiler_params=_params("arbitrary"),
        name="deltanet",
    )(dn3, conv_w_t, a_row, dt_row, g_row, incl, strict, levels)


def _ld_halves(ref, rows):
    return jnp.concatenate([ref[0, rows, :], ref[1, rows, :]], axis=1)


def _st_halves(ref, rows, val):
    ref[0, rows, :] = val[:, 0:128]
    ref[1, rows, :] = val[:, 128:256]


def _dil_body(q_ref, kc_ref, kp_ref, vc_ref, vp_ref, o_ref, qx_ref, kx_ref, vx_ref, m_ref, l_ref, acc_ref):
    n = pl.program_id(1)
    sup = DIL_SUPER
    blk = DIL_BLOCK
    _st_halves(qx_ref, slice(0, sup), q_ref[...])
    _st_halves(kx_ref, slice(0, sup), kp_ref[...])
    _st_halves(kx_ref, slice(sup, 2 * sup), kc_ref[...])
    _st_halves(vx_ref, slice(0, sup), vp_ref[...])
    _st_halves(vx_ref, slice(sup, 2 * sup), vc_ref[...])

    head_of_lane = _iota((1, BRANCH_WIDTH), 1) // HEAD_DIM
    qi = _iota((N_HEADS * blk, 2 * blk), 0) % blk
    kj = _iota((N_HEADS * blk, 2 * blk), 1)
    dist = qi + blk - kj
    band = (dist >= 0) & (dist <= DIL_KEYS)
    in_current = kj >= blk

    def stack(t):
        return jnp.concatenate([jnp.where(head_of_lane == h, t, jnp.zeros_like(t)) for h in range(N_HEADS)], axis=0)

    def per_head_lanes(col):
        out = jnp.broadcast_to(col[0:blk], (blk, BRANCH_WIDTH))
        for h in range(1, N_HEADS):
            out = jnp.where(head_of_lane == h, col[h * blk:(h + 1) * blk], out)
        return out

    def query_block(qb, p, dil):
        r = qb % dil
        m = qb // dil
        q_start = r + dil * blk * m
        k_start = sup + r + dil * blk * (m - 1)
        if dil == 1:
            q_rows = pl.ds(q_start, blk)
            k_rows = pl.ds(k_start, 2 * blk)
        else:
            q_rows = pl.ds(q_start, blk, stride=dil)
            k_rows = pl.ds(k_start, 2 * blk, stride=dil)
        q = stack(_ld_halves(qx_ref, q_rows).astype(BF16))
        k = _ld_halves(kx_ref, k_rows).astype(BF16)
        v = stack(_ld_halves(vx_ref, k_rows).astype(BF16))
        s = _dot_nt(q, k)
        yield
        has_prev = jnp.logical_or(n > 0, m > 0)
        s = jnp.where(band & jnp.logical_or(in_current, has_prev), s, -jnp.inf)
        mx = jnp.max(s, axis=-1, keepdims=True)
        pr = jnp.exp(s - mx)
        den = jnp.sum(pr, axis=-1, keepdims=True)
        pr = pr.astype(BF16)
        num = _dot(jnp.concatenate([pr[h * blk:(h + 1) * blk] for h in range(N_HEADS)], axis=1), v)
        yield
        mx_t = per_head_lanes(mx)
        den_t = per_head_lanes(den)
        if p == 0:
            _st_halves(m_ref, q_rows, mx_t)
            _st_halves(l_ref, q_rows, den_t)
            _st_halves(acc_ref, q_rows, num)
        else:
            m_old = _ld_halves(m_ref, q_rows)
            m_new = jnp.maximum(m_old, mx_t)
            a_old = jnp.exp(m_old - m_new)
            a_new = jnp.exp(mx_t - m_new)
            _st_halves(m_ref, q_rows, m_new)
            _st_halves(l_ref, q_rows, a_old * _ld_halves(l_ref, q_rows) + a_new * den_t)
            _st_halves(acc_ref, q_rows, a_old * _ld_halves(acc_ref, q_rows) + a_new * num)

    for p, dil in enumerate(DIL_DILATIONS):
        def blocks(i, carry, p=p, dil=dil):
            group = [query_block(DIL_GROUP * i + g, p, dil) for g in range(DIL_GROUP)]
            for _ in itertools.zip_longest(*group):
                pass
            return carry

        lax.fori_loop(0, sup // blk // DIL_GROUP, blocks, 0)

    whole = slice(0, sup)
    o_ref[...] = (_ld_halves(acc_ref, whole) / _ld_halves(l_ref, whole)).astype(o_ref.dtype)


def _dilated(dil3):
    b, s, _ = dil3.shape
    sup = DIL_SUPER
    cur = lambda blk: pl.BlockSpec((None, sup, BRANCH_WIDTH), lambda bi, i: (bi, i, blk))
    prev = lambda blk: pl.BlockSpec((None, sup, BRANCH_WIDTH), lambda bi, i: (bi, jnp.maximum(i - 1, 0), blk))
    return pl.pallas_call(
        _dil_body,
        grid=(b, s // sup),
        in_specs=[cur(0), cur(1), prev(1), cur(2), prev(2)],
        out_specs=pl.BlockSpec((None, sup, BRANCH_WIDTH), lambda bi, i: (bi, i, 0)),
        out_shape=jax.ShapeDtypeStruct((b, s, BRANCH_WIDTH), BF16),
        scratch_shapes=[pltpu.VMEM((2, sup, 128), F32), pltpu.VMEM((2, 2 * sup, 128), F32),
                        pltpu.VMEM((2, 2 * sup, 128), F32), pltpu.VMEM((2, sup, 128), F32),
                        pltpu.VMEM((2, sup, 128), F32), pltpu.VMEM((2, sup, 128), F32)],
        compiler_params=_params("parallel", "arbitrary"),
        name="dilated",
    )(dil3, dil3, dil3, dil3, dil3)


def _merge_body(x_ref, g_ref, wg_ref, o0_ref, o1_ref, o2_ref, o3_ref, wb_ref, wo_ref, out_ref):
    x = x_ref[...]
    d_model = x.shape[-1]
    h = (x * lax.rsqrt(jnp.mean(x * x, axis=-1, keepdims=True) + EPS) * g_ref[...]).astype(BF16)
    merged = None
    for nb, o_ref in enumerate((o0_ref, o1_ref, o2_ref, o3_ref)):
        gate = jax.nn.sigmoid(_dot(h, wg_ref[:, nb * d_model:(nb + 1) * d_model]))
        term = gate * _dot(o_ref[...], wb_ref[nb])
        merged = term if merged is None else merged + term
    out_ref[...] = x + _dot(merged.astype(BF16), wo_ref[...])


def _merge(x2, g_row, w_gate, branches, w_branch, w_out):
    t, d_model = x2.shape
    tm = ROW_TILE
    row = lambda w: pl.BlockSpec((tm, w), lambda i: (i, 0))
    return pl.pallas_call(
        _merge_body,
        grid=(t // tm,),
        in_specs=[row(d_model), _resident((1, d_model)), _resident((d_model, N_BRANCH * d_model))]
        + [row(BRANCH_WIDTH)] * N_BRANCH
        + [_resident((N_BRANCH, BRANCH_WIDTH, d_model)), _resident((d_model, d_model))],
        out_specs=row(d_model),
        out_shape=jax.ShapeDtypeStruct((t, d_model), F32),
        compiler_params=_params("parallel"),
        name="merge_out",
    )(x2, g_row, w_gate, *branches, w_branch, w_out)


def _mlp_body(x_ref, g_ref, w1_ref, w2_ref, out_ref):
    x = x_ref[...]
    h = (x * lax.rsqrt(jnp.mean(x * x, axis=-1, keepdims=True) + EPS) * g_ref[...]).astype(BF16)
    acc = x
    for c0 in range(0, w1_ref.shape[1], FF_TILE):
        a = jnp.maximum(_dot(h, w1_ref[:, c0:c0 + FF_TILE]), 0.0)
        acc = acc + _dot((a * a).astype(BF16), w2_ref[c0:c0 + FF_TILE, :])
    out_ref[...] = acc


def _mlp(x2, g_row, w1, w2):
    t, d_model = x2.shape
    d_ff = w1.shape[1]
    tm = ROW_TILE
    row = pl.BlockSpec((tm, d_model), lambda i: (i, 0))
    return pl.pallas_call(
        _mlp_body,
        grid=(t // tm,),
        in_specs=[row, _resident((1, d_model)), _resident((d_model, d_ff)), _resident((d_ff, d_model))],
        out_specs=row,
        out_shape=jax.ShapeDtypeStruct((t, d_model), F32),
        compiler_params=_params("parallel"),
        name="mlp",
    )(x2, g_row, w1, w2)


def _mixer_weights(w_in):
    d_model = w_in.shape[0]
    bw, kw = BRANCH_WIDTH, GLA_KEY_WIDTH
    o = 0
    sb = w_in[:, o:o + 3 * bw]
    o += 3 * bw
    gla_q, gla_k = w_in[:, o:o + kw], w_in[:, o + kw:o + 2 * kw]
    o += 2 * kw
    gla_v = w_in[:, o:o + bw]
    o += bw
    gla_lr = w_in[:, o:o + GLA_LOW_RANK]
    o += GLA_LOW_RANK
    gla_r = w_in[:, o:o + bw]
    o += bw
    dn_qkv = w_in[:, o:o + 3 * bw]
    o += 3 * bw
    dn_ab = w_in[:, o:o + 2 * N_HEADS]
    o += 2 * N_HEADS
    dn_gate = w_in[:, o:o + bw]
    o += bw
    dil = w_in[:, o:o + 3 * bw]
    o += 3 * bw
    zeros = lambda n: jnp.zeros((d_model, n), w_in.dtype)
    gla = jnp.concatenate([gla_q, gla_k, gla_v, gla_r, gla_lr, zeros(GLA_W - 3 * bw - GLA_LOW_RANK)], axis=1)
    dn = jnp.concatenate([dn_qkv, dn_gate, dn_ab, zeros(DN_W - 4 * bw - 2 * N_HEADS)], axis=1)
    return jnp.concatenate([sb, gla, dn, dil], axis=1).astype(BF16), o


def _rope_tables(seq):
    half = ROT_DIM // 2
    inv_freq = ROPE_THETA ** (-jnp.arange(half, dtype=F32) / half)
    ang = jnp.arange(seq, dtype=jnp.int32).astype(F32)[:, None] * inv_freq[None, :]
    cos, sin = jnp.cos(ang), jnp.sin(ang)
    rest = HEAD_DIM - ROT_DIM
    cos_h = jnp.concatenate([cos, cos, jnp.ones((seq, rest), F32)], axis=1)
    sin_h = jnp.concatenate([-sin, sin, jnp.zeros((seq, rest), F32)], axis=1)
    return jnp.tile(cos_h, (1, N_HEADS)), jnp.tile(sin_h, (1, N_HEADS))


def kernel(x, norm_mix_g, norm_mlp_g, w_in, gla_w_lr2, gla_b_lr, gla_norm_g, dn_conv_w, dn_a_log, dn_dt_bias,
           dn_norm_g, dil_q_norm_g, dil_k_norm_g, w_branch, w_out, w_mlp_in, w_mlp_out):
    b, s, d_model = x.shape
    depth = w_in.shape[0]
    t = b * s
    assert t % ROW_TILE == 0 and s % ROW_TILE == 0 and s % DIL_SUPER == 0 and s % SB_TILE == 0
    cos_t, sin_t = _rope_tables(s)
    tile_heads = lambda g: jnp.tile(g, N_HEADS)[None, :]
    pad_lanes = lambda v: jnp.pad(v, (0, 128 - v.shape[0]))[None, :]

    x2 = x.reshape(t, d_model)
    for l in range(depth):
        w_mix, n_mix = _mixer_weights(w_in[l])
        w_gate = w_in[l][:, n_mix:].astype(BF16)
        sb, gla, dn, dil = _in_proj(x2, norm_mix_g[l][None, :], w_mix, cos_t, sin_t,
                                    tile_heads(dil_q_norm_g[l]), tile_heads(dil_k_norm_g[l]), s)
        o_sb = _stick_breaking(sb.reshape(b, s, SB_W))
        w2_pad = jnp.pad(gla_w_lr2[l], ((0, GLA_KEY_WIDTH - GLA_LOW_RANK), (0, 0)))
        o_gla = _gla(gla.reshape(b, s, GLA_W), w2_pad, gla_b_lr[l][None, :], tile_heads(gla_norm_g[l]))
        o_dn = _deltanet(dn.reshape(b, s, DN_W), dn_conv_w[l].T, pad_lanes(-jnp.exp(dn_a_log[l])),
                         pad_lanes(dn_dt_bias[l]), tile_heads(dn_norm_g[l]))
        o_dil = _dilated(dil.reshape(b, s, DIL_W))
        branches = [o.reshape(t, BRANCH_WIDTH) for o in (o_sb, o_gla, o_dn, o_dil)]
        x2 = _merge(x2, norm_mix_g[l][None, :], w_gate, branches, w_branch[l].astype(BF16), w_out[l].astype(BF16))
        x2 = _mlp(x2, norm_mlp_g[l][None, :], w_mlp_in[l].astype(BF16), w_mlp_out[l].astype(BF16))
    return x2.reshape(b, s, d_model)
```

```python
import itertools

import jax
import jax.numpy as jnp
from jax import lax
from jax.experimental import pallas as pl
from jax.experimental.pallas import tpu as pltpu

F32 = jnp.float32
BF16 = jnp.bfloat16

HEAD_DIM = 64
N_HEADS = 4
BRANCH_WIDTH = N_HEADS * HEAD_DIM
GLA_KEY_DIM = 32
GLA_LOW_RANK = 16
GLA_GATE_NORM = 16.0
GLA_KEY_WIDTH = N_HEADS * GLA_KEY_DIM
DN_CONV = 4
DIL_DILATIONS = (1, 4, 16)
DIL_KEYS = 128
ROPE_THETA = 500000.0
ROT_DIM = HEAD_DIM // 4
N_BRANCH = 4
EPS = 1e-6
LOG2_E = 1.4426950408889634

SB_W = 3 * BRANCH_WIDTH
GLA_W = 896
DN_W = 1152
DIL_W = 3 * BRANCH_WIDTH
MIX_W = SB_W + GLA_W + DN_W + DIL_W

ROW_TILE = 512
SB_TILE = 256
SB_RUN_STOP = 160.0
GLA_CHUNK = 64
GLA_SUB = 16
GLA_STEP = 256
DN_CHUNK = 64
DN_STEP = 128
DIL_BLOCK = 128
DIL_SUPER = DIL_BLOCK * DIL_DILATIONS[-1]
DIL_GROUP = 4
FF_TILE = 1024

V7X_VMEM_LIMIT = 56 * 1024 * 1024


def _dot(a, b):
    return jnp.dot(a, b, preferred_element_type=F32)


def _dot_nt(a, b):
    return lax.dot_general(a, b, (((1,), (1,)), ((), ())), preferred_element_type=F32)


def _dot_tn(a, b):
    return lax.dot_general(a, b, (((0,), (0,)), ((), ())), preferred_element_type=F32)


def _split2(a):
    hi = a.astype(BF16)
    lo = (a - hi.astype(F32)).astype(BF16)
    return hi, lo


def _split3(a):
    hi = a.astype(BF16)
    r = a - hi.astype(F32)
    mid = r.astype(BF16)
    lo = (r - mid.astype(F32)).astype(BF16)
    return hi, mid, lo


def _dot_exact_rhs(a, b_exact, dot=_dot, parts=2):
    pieces = _split2(a) if parts == 2 else _split3(a)
    out = dot(pieces[0], b_exact)
    for p in pieces[1:]:
        out = out + dot(p, b_exact)
    return out


def _iota(shape, dim):
    return lax.broadcasted_iota(jnp.int32, shape, dim)


def _group_ones(rows, cols, rgroup, cgroup):
    return (_iota((rows, cols), 0) // rgroup == _iota((rows, cols), 1) // cgroup).astype(BF16)


def _log_sigmoid(z):
    return jnp.minimum(z, 0.0) - jnp.log1p(jnp.exp(-jnp.abs(z)))


def _softplus(z):
    return jnp.maximum(z, 0.0) + jnp.log1p(jnp.exp(-jnp.abs(z)))


def _silu(z):
    return z * jax.nn.sigmoid(z)


def _head_rms(x, gain_row, head_ones):
    ms = _dot_exact_rhs(x * x, head_ones) * (1.0 / HEAD_DIM)
    return x * lax.rsqrt(ms + EPS) * gain_row


def _params(*semantics):
    return pltpu.CompilerParams(dimension_semantics=semantics, vmem_limit_bytes=V7X_VMEM_LIMIT)


def _resident(shape):
    nd = len(shape)
    return pl.BlockSpec(shape, lambda *_: (0,) * nd, pipeline_mode=pl.Buffered(1))


def _in_proj_body(x_ref, g_ref, w_ref, cos_ref, sin_ref, qg_ref, kg_ref, sb_ref, gla_ref, dn_ref, dil_ref):
    x = x_ref[...]
    h = (x * lax.rsqrt(jnp.mean(x * x, axis=-1, keepdims=True) + EPS) * g_ref[...]).astype(BF16)
    q_scale = jnp.where(_iota((1, SB_W), 1) < BRANCH_WIDTH, LOG2_E * HEAD_DIM ** -0.5, 1.0)
    sb_ref[...] = (_dot(h, w_ref[:, 0:SB_W]) * q_scale).astype(BF16)
    off = SB_W
    gla_ref[...] = _dot(h, w_ref[:, off:off + GLA_W])
    off += GLA_W
    dn_ref[...] = _dot(h, w_ref[:, off:off + DN_W])
    off += DN_W
    d = _dot(h, w_ref[:, off:off + DIL_W])

    head_ones = _group_ones(BRANCH_WIDTH, BRANCH_WIDTH, HEAD_DIM, HEAD_DIM)
    lane = _iota((1, BRANCH_WIDTH), 1) % HEAD_DIM
    first_half = lane < ROT_DIM // 2
    cos = cos_ref[...]
    sin = sin_ref[...]

    def norm_rope(t, gain_row):
        y = _head_rms(t, gain_row, head_ones)
        partner = jnp.where(first_half, pltpu.roll(y, BRANCH_WIDTH - ROT_DIM // 2, 1), pltpu.roll(y, ROT_DIM // 2, 1))
        return y * cos + partner * sin

    dil_ref[:, 0:BRANCH_WIDTH] = norm_rope(d[:, 0:BRANCH_WIDTH], qg_ref[...]) * (HEAD_DIM ** -0.5)
    dil_ref[:, BRANCH_WIDTH:2 * BRANCH_WIDTH] = norm_rope(d[:, BRANCH_WIDTH:2 * BRANCH_WIDTH], kg_ref[...])
    dil_ref[:, 2 * BRANCH_WIDTH:] = d[:, 2 * BRANCH_WIDTH:]


def _in_proj(x2, g_row, w_mix, cos_t, sin_t, qg_row, kg_row, seq):
    t, d_model = x2.shape
    tm = ROW_TILE
    n_seq_tiles = seq // tm
    row = lambda w: pl.BlockSpec((tm, w), lambda i: (i, 0))
    tab = pl.BlockSpec((tm, BRANCH_WIDTH), lambda i: (i % n_seq_tiles, 0))
    return pl.pallas_call(
        _in_proj_body,
        grid=(t // tm,),
        in_specs=[row(d_model), _resident((1, d_model)), _resident((d_model, MIX_W)), tab, tab,
                  _resident((1, BRANCH_WIDTH)), _resident((1, BRANCH_WIDTH))],
        out_specs=[row(SB_W), row(GLA_W), row(DN_W), row(DIL_W)],
        out_shape=[jax.ShapeDtypeStruct((t, SB_W), BF16), jax.ShapeDtypeStruct((t, GLA_W), F32),
                   jax.ShapeDtypeStruct((t, DN_W), F32), jax.ShapeDtypeStruct((t, DIL_W), F32)],
        compiler_params=_params("parallel"),
        name="in_proj",
    )(x2, g_row, w_mix, cos_t, sin_t, qg_row, kg_row)


def _sb_body(q_ref, k_ref, v_ref, o_ref, z_a, z_b, w_a, w_b):
    tq = q_ref.shape[0]
    qi = pl.program_id(1)
    q = q_ref[...]
    head_of_lane = _iota((1, BRANCH_WIDTH), 1) // HEAD_DIM
    q_stack = jnp.concatenate([jnp.where(head_of_lane == h, q, jnp.zeros_like(q)) for h in range(N_HEADS)], axis=0)
    rows = _iota((tq, tq), 0)
    cols = _iota((tq, tq), 1)
    from_here = (rows >= cols).astype(BF16)
    causal = jnp.concatenate([cols < rows] * N_HEADS, axis=0)
    sign_bit = jnp.uint32(0x80000000)

    def scores(j):
        return _dot_nt(q_stack, k_ref[pl.ds(pl.multiple_of(j * tq, tq), tq), :])

    def weights(z_ref, run, masked):
        z = z_ref[...]
        neg_abs = lax.bitcast_convert_type(lax.bitcast_convert_type(z, jnp.uint32) | sign_bit, F32)
        sp = jnp.maximum(z, 0.0) + jnp.log(1.0 + jnp.exp2(neg_abs)) * LOG2_E
        if masked:
            sp = jnp.where(causal, sp, 0.0)
        tail = _dot(sp.astype(BF16), from_here)
        w = jnp.exp2((z_ref[...] - run) - tail)
        if masked:
            w = jnp.where(causal, w, 0.0)
        w = w.astype(BF16)
        w_cat = jnp.concatenate([w[h * tq:(h + 1) * tq] for h in range(N_HEADS)], axis=1)
        return w_cat, run + tail[:, 0:1]

    def apply(w_cat, j):
        vb = v_ref[pl.ds(pl.multiple_of(j * tq, tq), tq), :]
        v_stack = jnp.concatenate([jnp.where(head_of_lane == h, vb, jnp.zeros_like(vb)) for h in range(N_HEADS)], axis=0)
        return _dot(w_cat, v_stack)

    def step(j, z_cur, z_nxt, w_cur, w_prv, acc, run, masked=False):
        if w_prv is not None:
            acc = acc + apply(w_prv[...], jnp.maximum(j + 1, 0))
        z_nxt[...] = scores(jnp.maximum(j - 1, 0))
        w_cat, run = weights(z_cur, run + jnp.where(j < 0, jnp.inf, 0.0), masked)
        w_cur[...] = w_cat
        return acc, run

    z_b[...] = scores(qi)
    acc, run = step(qi, z_b, z_a, w_b, None, jnp.zeros((tq, BRANCH_WIDTH), F32),
                    jnp.zeros((N_HEADS * tq, 1), F32), masked=True)
    acc, run = step(qi - 1, z_a, z_b, w_a, w_b, acc, run)

    def more(carry):
        j, _, run = carry
        return jnp.logical_and(j >= 0, jnp.min(run) < SB_RUN_STOP)

    def pair(carry):
        j, acc, run = carry
        acc, run = step(j, z_b, z_a, w_b, w_a, acc, run)
        acc, run = step(j - 1, z_a, z_b, w_a, w_b, acc, run)
        return j - 2, acc, run

    j, acc, run = lax.while_loop(more, pair, (qi - 2, acc, run))
    o_ref[...] = (acc + apply(w_a[...], jnp.maximum(j + 1, 0))).astype(o_ref.dtype)


def _stick_breaking(sb3):
    b, s, _ = sb3.shape
    tq = SB_TILE
    whole = lambda blk: pl.BlockSpec((None, s, BRANCH_WIDTH), lambda bi, i: (bi, 0, blk), pipeline_mode=pl.Buffered(1))
    return pl.pallas_call(
        _sb_body,
        grid=(b, s // tq),
        in_specs=[pl.BlockSpec((None, tq, BRANCH_WIDTH), lambda bi, i: (bi, i, 0)), whole(1), whole(2)],
        out_specs=pl.BlockSpec((None, tq, BRANCH_WIDTH), lambda bi, i: (bi, i, 0)),
        out_shape=jax.ShapeDtypeStruct((b, s, BRANCH_WIDTH), BF16),
        scratch_shapes=[pltpu.VMEM((N_HEADS * tq, tq), F32), pltpu.VMEM((N_HEADS * tq, tq), F32),
                        pltpu.VMEM((tq, N_HEADS * tq), BF16), pltpu.VMEM((tq, N_HEADS * tq), BF16)],
        compiler_params=_params("parallel", "arbitrary"),
        name="stick_breaking",
    )(sb3, sb3, sb3)


def _gla_body(x_ref, w2_ref, b2_ref, g_ref, o_ref, state_ref):
    c = GLA_CHUNK
    nsub = c // GLA_SUB
    n_rows = x_ref.shape[0]
    n_chunks = GLA_STEP // c

    @pl.when(pl.program_id(0) == 0)
    def _():
        state_ref[...] = jnp.zeros_like(state_ref)

    ri = _iota((c, c), 0)
    ci = _iota((c, c), 1)
    tril_incl = (ci <= ri).astype(BF16)
    earlier_sub = (ci // GLA_SUB) < (ri // GLA_SUB)
    key_head = _iota((1, GLA_KEY_WIDTH), 1) // GLA_KEY_DIM
    val_head = _iota((1, BRANCH_WIDTH), 1) // HEAD_DIM
    key_to_val = _group_ones(GLA_KEY_WIDTH, BRANCH_WIDTH, GLA_KEY_DIM, HEAD_DIM)
    state_mask = key_to_val > 0
    head_ones = _group_ones(BRANCH_WIDTH, BRANCH_WIDTH, HEAD_DIM, HEAD_DIM)
    ones_cv = jnp.ones((c, BRANCH_WIDTH), BF16)
    sub_pos = _iota((1, GLA_SUB, 1), 1)
    scale = GLA_KEY_DIM ** -0.5

    within = {}

    def chunk_without_state(b, ic):
        rows = slice(ic * c, (ic + 1) * c)
        q = x_ref[b, rows, 0:128] * scale
        k = x_ref[b, rows, 128:256]
        v = x_ref[b, rows, 256:512]
        lr = x_ref[b, rows, 768:896]
        zg = jnp.dot(lr, w2_ref[...], precision=lax.Precision.HIGHEST, preferred_element_type=F32) + b2_ref[...]
        yield
        la = _log_sigmoid(zg) * (1.0 / GLA_GATE_NORM)
        bcum = _dot_exact_rhs(la, tril_incl, dot=lambda p, m: _dot(m, p), parts=3)
        btot_kv = _dot_exact_rhs(la, ones_cv, dot=_dot_tn, parts=3)
        yield
        b_last = bcum[c - 1:c, :]
        qe_b = (q * jnp.exp(bcum)).astype(BF16)
        ke_b = (k * jnp.exp(-bcum)).astype(BF16)
        kl = k * jnp.exp(b_last - bcum)
        state_add = jnp.where(state_mask, _dot_tn(kl.astype(BF16), v.astype(BF16)), 0.0)

        scores = [_dot_nt(jnp.where(key_head == h, qe_b, jnp.zeros_like(qe_b)), ke_b) for h in range(N_HEADS)]
        yield
        p_cat = jnp.concatenate([jnp.where(earlier_sub, sc, 0.0).astype(BF16) for sc in scores], axis=1)
        v_stack = jnp.concatenate([jnp.where(val_head == h, v, 0.0).astype(BF16) for h in range(N_HEADS)], axis=0)
        o = _dot(p_cat, v_stack)

        q3 = q.reshape(nsub, GLA_SUB, GLA_KEY_WIDTH)
        k3 = k.reshape(nsub, GLA_SUB, GLA_KEY_WIDTH)
        b3 = bcum.reshape(nsub, GLA_SUB, GLA_KEY_WIDTH)
        v3 = v.reshape(nsub, GLA_SUB, BRANCH_WIDTH)
        for jj in range(GLA_SUB):
            kj = k3[:, jj:jj + 1, :]
            bj = b3[:, jj:jj + 1, :]
            vj = jnp.broadcast_to(v3[:, jj:jj + 1, :], v3.shape).reshape(c, BRANCH_WIDTH)
            dec = jnp.exp(jnp.where(sub_pos >= jj, b3 - bj, -jnp.inf))
            pj = (q3 * kj * dec).reshape(c, GLA_KEY_WIDTH).astype(BF16)
            o = o + _dot(pj, key_to_val) * vj
        within[b, ic] = (qe_b, o, jnp.exp(btot_kv), state_add)

    todo = [chunk_without_state(b, ic) for ic in range(n_chunks) for b in range(n_rows)]
    for _ in itertools.zip_longest(*todo):
        pass

    states = [state_ref[b] for b in range(n_rows)]
    for ic in range(n_chunks):
        for b in range(n_rows):
            qe_b, o, decay_kv, state_add = within[b, ic]
            rows = slice(ic * c, (ic + 1) * c)
            o = o + _dot(qe_b, states[b].astype(BF16))
            states[b] = states[b] * decay_kv + state_add
            y = _head_rms(o, g_ref[...], head_ones) * _silu(x_ref[b, rows, 512:768])
            o_ref[b, rows, :] = y.astype(o_ref.dtype)
    for b in range(n_rows):
        state_ref[b] = states[b]


def _gla(gla3, w2_pad, b2_row, g_row):
    b, s, _ = gla3.shape
    return pl.pallas_call(
        _gla_body,
        grid=(s // GLA_STEP,),
        in_specs=[pl.BlockSpec((b, GLA_STEP, GLA_W), lambda i: (0, i, 0)),
                  _resident((GLA_KEY_WIDTH, GLA_KEY_WIDTH)), _resident((1, GLA_KEY_WIDTH)),
                  _resident((1, BRANCH_WIDTH))],
        out_specs=pl.BlockSpec((b, GLA_STEP, BRANCH_WIDTH), lambda i: (0, i, 0)),
        out_shape=jax.ShapeDtypeStruct((b, s, BRANCH_WIDTH), BF16),
        scratch_shapes=[pltpu.VMEM((b, GLA_KEY_WIDTH, BRANCH_WIDTH), F32)],
        compiler_params=_params("arbitrary"),
        name="gla",
    )(gla3, w2_pad, b2_row, g_row)


def _dn_body(x_ref, cw_ref, a_ref, dt_ref, g_ref, incl_ref, strict_ref, level_ref, o_ref, xbuf_ref, state_ref):
    c = DN_CHUNK
    n = N_HEADS * c
    qkv_w = 3 * BRANCH_WIDTH

    @pl.when(pl.program_id(0) == 0)
    def _():
        state_ref[...] = jnp.zeros_like(state_ref)
        xbuf_ref[:, 0:8, :] = jnp.zeros((x_ref.shape[0], 8, qkv_w), F32)

    head_ones = _group_ones(BRANCH_WIDTH, BRANCH_WIDTH, HEAD_DIM, HEAD_DIM)
    head_of_lane = _iota((1, BRANCH_WIDTH), 1) // HEAD_DIM
    ri = _iota((c, c), 0)
    ci = _iota((c, c), 1)
    cum_incl = (ci <= ri).astype(BF16)
    cum_incl_t = (ri <= ci).astype(BF16)
    eye = (_iota((n, n), 0) == _iota((n, n), 1)).astype(F32)

    def l2n(t):
        return t * lax.rsqrt(_dot_exact_rhs(t * t, head_ones) + EPS)

    def stack(t):
        return jnp.concatenate([jnp.where(head_of_lane == h, t, 0.0) for h in range(N_HEADS)], axis=0)

    def stack_col(t, lane0):
        return jnp.concatenate([t[:, lane0 + h:lane0 + h + 1] for h in range(N_HEADS)], axis=0)

    within = {}

    def chunk_without_state(b, ic):
        rows = slice(ic * c, (ic + 1) * c)
        xbuf_ref[b, 8:8 + c, :] = x_ref[b, rows, 0:qkv_w]
        conv = cw_ref[DN_CONV - 1:DN_CONV, :] * xbuf_ref[b, 8:8 + c, :]
        for i in range(DN_CONV - 1):
            conv = conv + cw_ref[i:i + 1, :] * xbuf_ref[b, pl.ds(8 - (DN_CONV - 1) + i, c), :]
        xbuf_ref[b, 0:8, :] = xbuf_ref[b, c:c + 8, :]
        act = _silu(conv)
        q_all = l2n(act[:, 0:BRANCH_WIDTH]) * (HEAD_DIM ** -0.5)
        k_all = l2n(act[:, BRANCH_WIDTH:2 * BRANCH_WIDTH])
        v_all = act[:, 2 * BRANCH_WIDTH:]

        ab = x_ref[b, rows, qkv_w + BRANCH_WIDTH:]
        g_tok = a_ref[...] * _softplus(ab + dt_ref[...])
        beta_all = jax.nn.sigmoid(ab)
        g_col = _dot_exact_rhs(g_tok, cum_incl, dot=lambda p, m: _dot(m, p), parts=3)
        g_row = _dot_exact_rhs(g_tok, cum_incl_t, dot=_dot_tn, parts=3)
        yield

        gc = stack_col(g_col, 0)
        gr = jnp.concatenate([g_row[h:h + 1, :] for h in range(N_HEADS)], axis=1)
        g_last = jnp.concatenate([jnp.broadcast_to(g_col[c - 1:c, h:h + 1], (c, 1)) for h in range(N_HEADS)], axis=0)
        beta = stack_col(beta_all, N_HEADS)
        egc = jnp.exp(gc)
        q_s, k_s, v_s = stack(q_all), stack(k_all), stack(v_all)
        kb_s = k_s * beta
        decay = jnp.exp(jnp.minimum(gc - gr, 0.0))
        k_b = k_s.astype(BF16)
        lower = _dot_nt(kb_s.astype(BF16), k_b) * decay * strict_ref[...]
        scores = _dot_nt(q_s.astype(BF16), k_b) * decay * incl_ref[...]
        yield

        tinv = eye - lower * level_ref[0]
        for lvl in range(1, level_ref.shape[0]):
            t_b = tinv.astype(BF16)
            half = _dot(t_b, (lower * level_ref[lvl]).astype(BF16)).astype(BF16)
            yield
            tinv = tinv - _dot(half, t_b)
            yield

        rhs = jnp.concatenate([v_s * beta, kb_s * egc], axis=1).astype(BF16)
        sol = _dot(tinv.astype(BF16), rhs)
        u, w = sol[:, 0:BRANCH_WIDTH], sol[:, BRANCH_WIDTH:]
        q_in = (q_s * egc).astype(BF16)
        k_up = (k_s * jnp.exp(g_last - gc)).astype(BF16)
        within[b, ic] = (u, w.astype(BF16), q_in, k_up, scores.astype(BF16), jnp.exp(g_last))

    n_rows = x_ref.shape[0]
    n_chunks = x_ref.shape[1] // c
    todo = [chunk_without_state(b, ic) for ic in range(n_chunks) for b in range(n_rows)]
    for _ in itertools.zip_longest(*todo):
        pass

    states = [state_ref[b] for b in range(n_rows)]
    for ic in range(n_chunks):
        rows = slice(ic * c, (ic + 1) * c)
        states_b = [s.astype(BF16) for s in states]
        v_new = [within[b, ic][0] - _dot(within[b, ic][1], states_b[b]) for b in range(n_rows)]
        v_new_b = [t.astype(BF16) for t in v_new]
        for b in range(n_rows):
            u, w, q_in, k_up, scores, a_last = within[b, ic]
            o_s = _dot(q_in, states_b[b]) + _dot(scores, v_new_b[b])
            states[b] = a_last * states[b] + _dot_tn(k_up, v_new_b[b])
            o = o_s[0:c]
            for h in range(1, N_HEADS):
                o = o + o_s[h * c:(h + 1) * c]
            gate = x_ref[b, rows, qkv_w:qkv_w + BRANCH_WIDTH]
            o_ref[b, rows, :] = (_head_rms(o, g_ref[...], head_ones) * _silu(gate)).astype(o_ref.dtype)
    for b in range(n_rows):
        state_ref[b] = states[b]


def _dn_masks():
    n = N_HEADS * DN_CHUNK
    r = jnp.arange(n)[:, None]
    cidx = jnp.arange(n)[None, :]
    same_head = r // DN_CHUNK == cidx // DN_CHUNK
    incl = (same_head & (cidx <= r)).astype(F32)
    strict = (same_head & (cidx < r)).astype(F32)
    levels = []
    s = 1
    while s < DN_CHUNK:
        levels.append(((r // (2 * s) == cidx // (2 * s)) & (r % (2 * s) >= s) & (cidx % (2 * s) < s)).astype(F32))
        s *= 2
    return incl, strict, jnp.stack(levels)


def _deltanet(dn3, conv_w_t, a_row, dt_row, g_row):
    b, s, _ = dn3.shape
    c = DN_CHUNK
    n = N_HEADS * c
    incl, strict, levels = _dn_masks()
    assert c == HEAD_DIM
    return pl.pallas_call(
        _dn_body,
        grid=(s // DN_STEP,),
        in_specs=[pl.BlockSpec((b, DN_STEP, DN_W), lambda i: (0, i, 0)),
                  _resident((DN_CONV, 3 * BRANCH_WIDTH)), _resident((1, 128)), _resident((1, 128)),
                  _resident((1, BRANCH_WIDTH)), _resident((n, n)), _resident((n, n)), _resident(levels.shape)],
        out_specs=pl.BlockSpec((b, DN_STEP, BRANCH_WIDTH), lambda i: (0, i, 0)),
        out_shape=jax.ShapeDtypeStruct((b, s, BRANCH_WIDTH), BF16),
        scratch_shapes=[pltpu.VMEM((b, c + 8, 3 * BRANCH_WIDTH), F32), pltpu.VMEM((b, n, BRANCH_WIDTH), F32)],
        compiler_params=_params("arbitrary"),
        name="deltanet",
    )(dn3, conv_w_t, a_row, dt_row, g_row, incl, strict, levels)


def _ld_halves(ref, rows):
    return jnp.concatenate([ref[0, rows, :], ref[1, rows, :]], axis=1)


def _st_halves(ref, rows, val):
    ref[0, rows, :] = val[:, 0:128]
    ref[1, rows, :] = val[:, 128:256]


def _dil_body(q_ref, kc_ref, kp_ref, vc_ref, vp_ref, o_ref, qx_ref, kx_ref, vx_ref, m_ref, l_ref, acc_ref):
    n = pl.program_id(1)
    sup = DIL_SUPER
    blk = DIL_BLOCK
    _st_halves(qx_ref, slice(0, sup), q_ref[...])
    _st_halves(kx_ref, slice(0, sup), kp_ref[...])
    _st_halves(kx_ref, slice(sup, 2 * sup), kc_ref[...])
    _st_halves(vx_ref, slice(0, sup), vp_ref[...])
    _st_halves(vx_ref, slice(sup, 2 * sup), vc_ref[...])

    head_of_lane = _iota((1, BRANCH_WIDTH), 1) // HEAD_DIM
    qi = _iota((N_HEADS * blk, 2 * blk), 0) % blk
    kj = _iota((N_HEADS * blk, 2 * blk), 1)
    dist = qi + blk - kj
    band = (dist >= 0) & (dist <= DIL_KEYS)
    in_current = kj >= blk

    def stack(t):
        return jnp.concatenate([jnp.where(head_of_lane == h, t, jnp.zeros_like(t)) for h in range(N_HEADS)], axis=0)

    def per_head_lanes(col):
        out = jnp.broadcast_to(col[0:blk], (blk, BRANCH_WIDTH))
        for h in range(1, N_HEADS):
            out = jnp.where(head_of_lane == h, col[h * blk:(h + 1) * blk], out)
        return out

    def query_block(qb, p, dil):
        r = qb % dil
        m = qb // dil
        q_start = r + dil * blk * m
        k_start = sup + r + dil * blk * (m - 1)
        if dil == 1:
            q_rows = pl.ds(pl.multiple_of(q_start, blk), blk)
            k_rows = pl.ds(pl.multiple_of(k_start, blk), 2 * blk)
        else:
            q_rows = pl.ds(q_start, blk, stride=dil)
            k_rows = pl.ds(k_start, 2 * blk, stride=dil)
        q = stack(_ld_halves(qx_ref, q_rows).astype(BF16))
        k = _ld_halves(kx_ref, k_rows).astype(BF16)
        v = stack(_ld_halves(vx_ref, k_rows).astype(BF16))
        s = _dot_nt(q, k)
        yield
        has_prev = jnp.logical_or(n > 0, m > 0)
        s = jnp.where(band & jnp.logical_or(in_current, has_prev), s, -jnp.inf)
        mx = jnp.max(s, axis=-1, keepdims=True)
        pr = jnp.exp(s - mx)
        den = jnp.sum(pr, axis=-1, keepdims=True)
        pr = pr.astype(BF16)
        num = _dot(jnp.concatenate([pr[h * blk:(h + 1) * blk] for h in range(N_HEADS)], axis=1), v)
        yield
        mx_t = per_head_lanes(mx)
        den_t = per_head_lanes(den)
        if p == 0:
            _st_halves(m_ref, q_rows, mx_t)
            _st_halves(l_ref, q_rows, den_t)
            _st_halves(acc_ref, q_rows, num)
            return
        m_old = _ld_halves(m_ref, q_rows)
        m_new = jnp.maximum(m_old, mx_t)
        a_old = jnp.exp(m_old - m_new)
        a_new = jnp.exp(mx_t - m_new)
        l_new = a_old * _ld_halves(l_ref, q_rows) + a_new * den_t
        acc_new = a_old * _ld_halves(acc_ref, q_rows) + a_new * num
        if p < len(DIL_DILATIONS) - 1:
            _st_halves(m_ref, q_rows, m_new)
            _st_halves(l_ref, q_rows, l_new)
            _st_halves(acc_ref, q_rows, acc_new)
        else:
            o_ref[q_rows, :] = (acc_new / l_new).astype(o_ref.dtype)

    order = sorted(DIL_DILATIONS, reverse=True)
    assert order[-1] == 1
    for p, dil in enumerate(order):
        def blocks(i, carry, p=p, dil=dil):
            group = [query_block(DIL_GROUP * i + g, p, dil) for g in range(DIL_GROUP)]
            for _ in itertools.zip_longest(*group):
                pass
            return carry

        lax.fori_loop(0, sup // blk // DIL_GROUP, blocks, 0)


def _dilated(dil3):
    b, s, _ = dil3.shape
    sup = DIL_SUPER
    cur = lambda blk: pl.BlockSpec((None, sup, BRANCH_WIDTH), lambda bi, i: (bi, i, blk))
    prev = lambda blk: pl.BlockSpec((None, sup, BRANCH_WIDTH), lambda bi, i: (bi, jnp.maximum(i - 1, 0), blk))
    return pl.pallas_call(
        _dil_body,
        grid=(b, s // sup),
        in_specs=[cur(0), cur(1), prev(1), cur(2), prev(2)],
        out_specs=pl.BlockSpec((None, sup, BRANCH_WIDTH), lambda bi, i: (bi, i, 0)),
        out_shape=jax.ShapeDtypeStruct((b, s, BRANCH_WIDTH), BF16),
        scratch_shapes=[pltpu.VMEM((2, sup, 128), F32), pltpu.VMEM((2, 2 * sup, 128), F32),
                        pltpu.VMEM((2, 2 * sup, 128), F32), pltpu.VMEM((2, sup, 128), F32),
                        pltpu.VMEM((2, sup, 128), F32), pltpu.VMEM((2, sup, 128), F32)],
        compiler_params=_params("parallel", "arbitrary"),
        name="dilated",
    )(dil3, dil3, dil3, dil3, dil3)


def _merge_body(x_ref, g_ref, wg_ref, o0_ref, o1_ref, o2_ref, o3_ref, wb_ref, wo_ref, out_ref):
    x = x_ref[...]
    d_model = x.shape[-1]
    h = (x * lax.rsqrt(jnp.mean(x * x, axis=-1, keepdims=True) + EPS) * g_ref[...]).astype(BF16)
    merged = None
    for nb, o_ref in enumerate((o0_ref, o1_ref, o2_ref, o3_ref)):
        gate = jax.nn.sigmoid(_dot(h, wg_ref[:, nb * d_model:(nb + 1) * d_model]))
        term = gate * _dot(o_ref[...], wb_ref[nb])
        merged = term if merged is None else merged + term
    out_ref[...] = x + _dot(merged.astype(BF16), wo_ref[...])


def _merge(x2, g_row, w_gate, branches, w_branch, w_out):
    t, d_model = x2.shape
    tm = ROW_TILE
    row = lambda w: pl.BlockSpec((tm, w), lambda i: (i, 0))
    return pl.pallas_call(
        _merge_body,
        grid=(t // tm,),
        in_specs=[row(d_model), _resident((1, d_model)), _resident((d_model, N_BRANCH * d_model))]
        + [row(BRANCH_WIDTH)] * N_BRANCH
        + [_resident((N_BRANCH, BRANCH_WIDTH, d_model)), _resident((d_model, d_model))],
        out_specs=row(d_model),
        out_shape=jax.ShapeDtypeStruct((t, d_model), F32),
        compiler_params=_params("parallel"),
        name="merge_out",
    )(x2, g_row, w_gate, *branches, w_branch, w_out)


def _mlp_body(x_ref, g_ref, w1_ref, w2_ref, out_ref):
    x = x_ref[...]
    h = (x * lax.rsqrt(jnp.mean(x * x, axis=-1, keepdims=True) + EPS) * g_ref[...]).astype(BF16)
    acc = x
    for c0 in range(0, w1_ref.shape[1], FF_TILE):
        a = jnp.maximum(_dot(h, w1_ref[:, c0:c0 + FF_TILE]), 0.0)
        acc = acc + _dot((a * a).astype(BF16), w2_ref[c0:c0 + FF_TILE, :])
    out_ref[...] = acc


def _mlp(x2, g_row, w1, w2):
    t, d_model = x2.shape
    d_ff = w1.shape[1]
    tm = ROW_TILE
    row = pl.BlockSpec((tm, d_model), lambda i: (i, 0))
    return pl.pallas_call(
        _mlp_body,
        grid=(t // tm,),
        in_specs=[row, _resident((1, d_model)), _resident((d_model, d_ff)), _resident((d_ff, d_model))],
        out_specs=row,
        out_shape=jax.ShapeDtypeStruct((t, d_model), F32),
        compiler_params=_params("parallel"),
        name="mlp",
    )(x2, g_row, w1, w2)


def _mixer_weights(w_in):
    d_model = w_in.shape[0]
    bw, kw = BRANCH_WIDTH, GLA_KEY_WIDTH
    o = 0
    sb = w_in[:, o:o + 3 * bw]
    o += 3 * bw
    gla_q, gla_k = w_in[:, o:o + kw], w_in[:, o + kw:o + 2 * kw]
    o += 2 * kw
    gla_v = w_in[:, o:o + bw]
    o += bw
    gla_lr = w_in[:, o:o + GLA_LOW_RANK]
    o += GLA_LOW_RANK
    gla_r = w_in[:, o:o + bw]
    o += bw
    dn_qkv = w_in[:, o:o + 3 * bw]
    o += 3 * bw
    dn_ab = w_in[:, o:o + 2 * N_HEADS]
    o += 2 * N_HEADS
    dn_gate = w_in[:, o:o + bw]
    o += bw
    dil = w_in[:, o:o + 3 * bw]
    o += 3 * bw
    zeros = lambda n: jnp.zeros((d_model, n), w_in.dtype)
    gla = jnp.concatenate([gla_q, gla_k, gla_v, gla_r, gla_lr, zeros(GLA_W - 3 * bw - GLA_LOW_RANK)], axis=1)
    dn = jnp.concatenate([dn_qkv, dn_gate, dn_ab, zeros(DN_W - 4 * bw - 2 * N_HEADS)], axis=1)
    return jnp.concatenate([sb, gla, dn, dil], axis=1).astype(BF16), o


def _rope_tables(seq):
    half = ROT_DIM // 2
    inv_freq = ROPE_THETA ** (-jnp.arange(half, dtype=F32) / half)
    ang = jnp.arange(seq, dtype=jnp.int32).astype(F32)[:, None] * inv_freq[None, :]
    cos, sin = jnp.cos(ang), jnp.sin(ang)
    rest = HEAD_DIM - ROT_DIM
    cos_h = jnp.concatenate([cos, cos, jnp.ones((seq, rest), F32)], axis=1)
    sin_h = jnp.concatenate([-sin, sin, jnp.zeros((seq, rest), F32)], axis=1)
    return jnp.tile(cos_h, (1, N_HEADS)), jnp.tile(sin_h, (1, N_HEADS))


def kernel(x, norm_mix_g, norm_mlp_g, w_in, gla_w_lr2, gla_b_lr, gla_norm_g, dn_conv_w, dn_a_log, dn_dt_bias,
           dn_norm_g, dil_q_norm_g, dil_k_norm_g, w_branch, w_out, w_mlp_in, w_mlp_out):
    b, s, d_model = x.shape
    depth = w_in.shape[0]
    t = b * s
    assert t % ROW_TILE == 0 and s % ROW_TILE == 0 and s % DIL_SUPER == 0 and s % SB_TILE == 0
    cos_t, sin_t = _rope_tables(s)
    tile_heads = lambda g: jnp.tile(g, N_HEADS)[None, :]
    pad_lanes = lambda v: jnp.pad(v, (0, 128 - v.shape[0]))[None, :]

    x2 = x.reshape(t, d_model)
    for l in range(depth):
        w_mix, n_mix = _mixer_weights(w_in[l])
        w_gate = w_in[l][:, n_mix:].astype(BF16)
        sb, gla, dn, dil = _in_proj(x2, norm_mix_g[l][None, :], w_mix, cos_t, sin_t,
                                    tile_heads(dil_q_norm_g[l]), tile_heads(dil_k_norm_g[l]), s)
        o_sb = _stick_breaking(sb.reshape(b, s, SB_W))
        w2_pad = jnp.pad(gla_w_lr2[l], ((0, GLA_KEY_WIDTH - GLA_LOW_RANK), (0, 0)))
        o_gla = _gla(gla.reshape(b, s, GLA_W), w2_pad, gla_b_lr[l][None, :], tile_heads(gla_norm_g[l]))
        o_dn = _deltanet(dn.reshape(b, s, DN_W), dn_conv_w[l].T, pad_lanes(-jnp.exp(dn_a_log[l])),
                         pad_lanes(dn_dt_bias[l]), tile_heads(dn_norm_g[l]))
        o_dil = _dilated(dil.reshape(b, s, DIL_W))
        branches = [o.reshape(t, BRANCH_WIDTH) for o in (o_sb, o_gla, o_dn, o_dil)]
        x2 = _merge(x2, norm_mix_g[l][None, :], w_gate, branches, w_branch[l].astype(BF16), w_out[l].astype(BF16))
        x2 = _mlp(x2, norm_mlp_g[l][None, :], w_mlp_in[l].astype(BF16), w_mlp_out[l].astype(BF16))
    return x2.reshape(b, s, d_model)
```
